```python
import jax, jax.numpy as jnp
from jax import lax
import numpy as np

D_MODEL = 1024
BATCH = 4
SEQ = 4096
DEPTH = 1

N_ATTN_HEADS = 8
ATTN_HEAD_DIM = 64
ATTN_WIDTH = N_ATTN_HEADS * ATTN_HEAD_DIM
Q_BLOCK = 128
FORGET_BIAS_OFFSET = 3.0
N_SGU_GROUPS = 8
SGU_GROUP_DIM = 64
SGU_WIDTH = N_SGU_GROUPS * SGU_GROUP_DIM
CHUNK = 128
N_BRANCHES = 2
D_FF = 2816
RMS_EPS = 1e-6
LN_EPS = 1e-5

IN_SIZES = (ATTN_WIDTH, ATTN_WIDTH, ATTN_WIDTH, N_ATTN_HEADS, SGU_WIDTH, SGU_WIDTH, N_BRANCHES * D_MODEL)
IN_WIDTH = sum(IN_SIZES)
SPLIT_POINTS = tuple(sum(IN_SIZES[:i + 1]) for i in range(len(IN_SIZES) - 1))

kernel_name = "macaron_fox_sgu_gated_hybrid"


def rms_norm(x, gain):
    x32 = x.astype(jnp.float32)
    y = x32 * lax.rsqrt(jnp.mean(x32 * x32, axis=-1, keepdims=True) + RMS_EPS)
    return y.astype(x.dtype) * gain


def layer_norm(x, gain, bias):
    x32 = x.astype(jnp.float32)
    mu = jnp.mean(x32, axis=-1, keepdims=True)
    var = jnp.mean(jnp.square(x32 - mu), axis=-1, keepdims=True)
    y = (x32 - mu) * lax.rsqrt(var + LN_EPS)
    return y.astype(x.dtype) * gain + bias


def swiglu(x, w_in, w_out):
    gate, up = jnp.split(x @ w_in, 2, axis=-1)
    return (jax.nn.silu(gate) * up) @ w_out


def forgetting_attention(q, k, v, f_logit):
    B, S, H, dh = q.shape
    nb = S // Q_BLOCK
    log_f = jax.nn.log_sigmoid(f_logit.astype(jnp.float32))
    c = jnp.cumsum(log_f, axis=1).transpose(0, 2, 1)
    qh = q.transpose(0, 2, 1, 3)
    kh = k.transpose(0, 2, 1, 3)
    vh = v.transpose(0, 2, 1, 3)
    q_blocks = qh.reshape(B, H, nb, Q_BLOCK, dh).transpose(2, 0, 1, 3, 4)
    c_blocks = c.reshape(B, H, nb, Q_BLOCK).transpose(2, 0, 1, 3)
    k_pos = jnp.arange(S)
    scale = ATTN_HEAD_DIM ** -0.5

    def one_block(args):
        q_blk, c_blk, blk = args
        q_pos = blk * Q_BLOCK + jnp.arange(Q_BLOCK)
        s = jnp.einsum('bhqd,bhkd->bhqk', q_blk, kh, preferred_element_type=jnp.float32) * scale
        s = s + (c_blk[..., :, None] - c[..., None, :])
        s = jnp.where(k_pos[None, :] <= q_pos[:, None], s, -jnp.inf)
        p = jax.nn.softmax(s, axis=-1)
        return jnp.einsum('bhqk,bhkd->bhqd', p.astype(vh.dtype), vh)

    o = lax.map(one_block, (q_blocks, c_blocks, jnp.arange(nb)))
    return o.transpose(1, 0, 3, 2, 4).reshape(B, S, H * dh)


def chunked_spatial_gating(u, v, ln_gain, ln_bias, w_spatial, b_spatial):
    B, S, _ = v.shape
    vn = layer_norm(v, ln_gain, ln_bias)
    vc = vn.reshape(B, S // CHUNK, CHUNK, N_SGU_GROUPS, SGU_GROUP_DIM)
    causal = jnp.tril(jnp.ones((CHUNK, CHUNK), dtype=bool))
    w = jnp.where(causal[None], w_spatial, jnp.zeros_like(w_spatial))
    mixed = jnp.einsum('gts,bnsgc->bntgc', w, vc) + b_spatial.T[:, :, None]
    return u * mixed.reshape(B, S, SGU_WIDTH)


def setup_inputs(seed: int = 0) -> dict:
    key = jax.random.key(seed)
    ks = jax.random.split(key, 20)
    f32 = jnp.float32

    def nrm(k, shape, scale):
        return jax.random.normal(k, shape, f32) * scale

    L = DEPTH
    return {
        "x": nrm(ks[0], (BATCH, SEQ, D_MODEL), 1.0),
        "ffn1_norm": 1.0 + nrm(ks[1], (L, D_MODEL), 0.05),
        "ffn1_w_in": nrm(ks[2], (L, D_MODEL, 2 * D_FF), D_MODEL ** -0.5),
        "ffn1_w_out": nrm(ks[3], (L, D_FF, D_MODEL), D_FF ** -0.5),
        "mix_norm": 1.0 + nrm(ks[4], (L, D_MODEL), 0.05),
        "w_in": nrm(ks[5], (L, D_MODEL, IN_WIDTH), D_MODEL ** -0.5),
        "b_forget": FORGET_BIAS_OFFSET + nrm(ks[6], (L, N_ATTN_HEADS), 0.1),
        "b_gate": nrm(ks[7], (L, N_BRANCHES * D_MODEL), 0.1),
        "sgu_ln_gain": 1.0 + nrm(ks[8], (L, SGU_WIDTH), 0.05),
        "sgu_ln_bias": nrm(ks[9], (L, SGU_WIDTH), 0.05),
        "sgu_w_spatial": nrm(ks[10], (L, N_SGU_GROUPS, CHUNK, CHUNK), CHUNK ** -0.5),
        "sgu_b_spatial": 1.0 + nrm(ks[11], (L, N_SGU_GROUPS, CHUNK), 0.1),
        "w_up_attn": nrm(ks[12], (L, ATTN_WIDTH, D_MODEL), ATTN_WIDTH ** -0.5),
        "w_up_sgu": nrm(ks[13], (L, SGU_WIDTH, D_MODEL), SGU_WIDTH ** -0.5),
        "w_out": nrm(ks[14], (L, D_MODEL, D_MODEL), D_MODEL ** -0.5),
        "ffn2_norm": 1.0 + nrm(ks[15], (L, D_MODEL), 0.05),
        "ffn2_w_in": nrm(ks[16], (L, D_MODEL, 2 * D_FF), D_MODEL ** -0.5),
        "ffn2_w_out": nrm(ks[17], (L, D_FF, D_MODEL), D_FF ** -0.5),
        "final_norm": 1.0 + nrm(ks[18], (D_MODEL,), 0.05),
    }


def reference(x, ffn1_norm, ffn1_w_in, ffn1_w_out, mix_norm, w_in, b_forget, b_gate,
              sgu_ln_gain, sgu_ln_bias, sgu_w_spatial, sgu_b_spatial, w_up_attn, w_up_sgu,
              w_out, ffn2_norm, ffn2_w_in, ffn2_w_out, final_norm):
    B, S, _ = x.shape
    h = x
    for l in range(DEPTH):
        h = h + 0.5 * swiglu(rms_norm(h, ffn1_norm[l]), ffn1_w_in[l], ffn1_w_out[l])
        n = rms_norm(h, mix_norm[l])
        z = n @ w_in[l]
        q, k, v, f, u, vg, g = jnp.split(z, SPLIT_POINTS, axis=-1)
        hs = (B, S, N_ATTN_HEADS, ATTN_HEAD_DIM)
        attn = forgetting_attention(q.reshape(hs), k.reshape(hs), v.reshape(hs), f + b_forget[l])
        sgu = chunked_spatial_gating(jax.nn.gelu(u, approximate=False), jax.nn.gelu(vg, approximate=False),
                                     sgu_ln_gain[l], sgu_ln_bias[l], sgu_w_spatial[l], sgu_b_spatial[l])
        g_attn, g_sgu = jnp.split(jax.nn.sigmoid(g + b_gate[l]), 2, axis=-1)
        merged = g_attn * (attn @ w_up_attn[l]) + g_sgu * (sgu @ w_up_sgu[l])
        h = h + merged @ w_out[l]
        h = h + 0.5 * swiglu(rms_norm(h, ffn2_norm[l]), ffn2_w_in[l], ffn2_w_out[l])
    return rms_norm(h, final_norm)
```

```python
import functools

import jax
import jax.numpy as jnp
from jax import lax
from jax.experimental import pallas as pl
from jax.experimental.pallas import tpu as pltpu

F32 = jnp.float32
BF16 = jnp.bfloat16

D_MODEL = 1024
D_FF = 2816
N_HEADS = 8
HEAD_DIM = 64
ATTN_WIDTH = N_HEADS * HEAD_DIM
N_GROUPS = 8
GROUP_DIM = 64
SGU_WIDTH = N_GROUPS * GROUP_DIM
CHUNK = 128
RMS_EPS = 1e-6
LN_EPS = 1e-5

LANES = 128
V7X_VMEM_BYTES = 64 * 1024 * 1024
VMEM_LIMIT = V7X_VMEM_BYTES * 7 // 8

TOKEN_TILE = 512
FFN_COLS = 256
ATTN_BLOCK = 256
V_ROWS = HEAD_DIM + 16
N_SPLIT = 3
QK_DEPTH = 2 * LANES


def _rms_norm(x, gain):
    return (x * lax.rsqrt(jnp.mean(x * x, axis=-1, keepdims=True) + RMS_EPS)) * gain


def _gelu(x):
    return x * (lax.erf(x * (2.0 ** -0.5)) + 1.0) * 0.5


def _resident(shape):
    zeros = (0,) * len(shape)
    return pl.BlockSpec(shape, lambda *_: zeros, pipeline_mode=pl.Buffered(1))


def _params():
    return pltpu.CompilerParams(dimension_semantics=("arbitrary",),
                                vmem_limit_bytes=VMEM_LIMIT)


def _ffn_kernel(x_ref, gain_ref, wg_ref, wu_ref, wo_ref, fgain_ref, o_ref, act_ref, *, final_norm):
    x = x_ref[...]
    nb = _rms_norm(x, gain_ref[...]).astype(BF16)
    for j in range(D_FF // FFN_COLS):
        cols = slice(j * FFN_COLS, (j + 1) * FFN_COLS)
        gate = jnp.dot(nb, wg_ref[:, cols], preferred_element_type=F32)
        up = jnp.dot(nb, wu_ref[:, cols], preferred_element_type=F32)
        act_ref[:, cols] = (gate * jax.nn.sigmoid(gate) * up).astype(BF16)
    y = jnp.dot(act_ref[...], wo_ref[...], preferred_element_type=F32)
    h = x + 0.5 * y
    if final_norm:
        h = _rms_norm(h, fgain_ref[...])
    o_ref[...] = h


def _ffn(x, gain, wg, wu, wo, fgain, *, final_norm):
    t = x.shape[0]
    tile = pl.BlockSpec((TOKEN_TILE, D_MODEL), lambda i: (i, 0))
    return pl.pallas_call(
        functools.partial(_ffn_kernel, final_norm=final_norm),
        grid=(t // TOKEN_TILE,),
        in_specs=[tile, _resident((1, D_MODEL)), _resident((D_MODEL, D_FF)),
                  _resident((D_MODEL, D_FF)), _resident((D_FF, D_MODEL)),
                  _resident((1, D_MODEL))],
        out_specs=tile,
        out_shape=jax.ShapeDtypeStruct((t, D_MODEL), F32),
        scratch_shapes=[pltpu.VMEM((TOKEN_TILE, D_FF), BF16)],
        compiler_params=_params(),
        name="ffn_final" if final_norm else "ffn",
    )(x, gain, wg, wu, wo, fgain)


def _split_bf16(x):
    parts = []
    for _ in range(N_SPLIT):
        p = x.astype(BF16).astype(F32)
        parts.append(p)
        x = x - p
    return parts


def _proj_kernel(h_ref, gain_ref, wqt_ref, wvt_ref, wn_ref, bf_ref, lng_ref, lnb_ref,
                 wsp_ref, bsp_ref,
                 qt_ref, k_ref, c_ref, v_ref, sgu_ref, carry_ref, *, tiles_per_seq):
    i = pl.program_id(0)

    @pl.when(i % tiles_per_seq == 0)
    def _():
        carry_ref[...] = jnp.zeros_like(carry_ref)

    nb = _rms_norm(h_ref[...], gain_ref[...]).astype(BF16)

    tdims = (((1,), (1,)), ((), ()))
    qt = lax.dot_general(wqt_ref[...], nb, tdims, preferred_element_type=F32)
    qt_ref[0] = qt.astype(BF16)
    vt = lax.dot_general(wvt_ref[...], nb, tdims, preferred_element_type=F32).astype(BF16)
    pad_row = lax.broadcasted_iota(jnp.int32, (V_ROWS - HEAD_DIM, ATTN_BLOCK), 0)
    ones_row = jnp.where(pad_row == 0, 1.0, 0.0).astype(BF16)
    for s in range(TOKEN_TILE // ATTN_BLOCK):
        for hd in range(N_HEADS):
            v_ref[0, hd, s, :HEAD_DIM, :] = vt[hd * HEAD_DIM:(hd + 1) * HEAD_DIM,
                                               s * ATTN_BLOCK:(s + 1) * ATTN_BLOCK]
            v_ref[0, hd, s, HEAD_DIM:, :] = ones_row

    z = jnp.dot(nb, wn_ref[...], preferred_element_type=F32)
    k_ref[0] = z[:, :ATTN_WIDTH].astype(BF16)
    u = z[:, ATTN_WIDTH:ATTN_WIDTH + SGU_WIDTH]
    vg = z[:, ATTN_WIDTH + SGU_WIDTH:ATTN_WIDTH + 2 * SGU_WIDTH]
    f = z[:, ATTN_WIDTH + 2 * SGU_WIDTH:] + bf_ref[...]

    logf = jnp.minimum(f, 0.0) - jnp.log1p(jnp.exp(-jnp.abs(f)))
    row = lax.broadcasted_iota(jnp.int32, (TOKEN_TILE, TOKEN_TILE), 0)
    col = lax.broadcasted_iota(jnp.int32, (TOKEN_TILE, TOKEN_TILE), 1)
    tri = jnp.where(row >= col, 1.0, 0.0).astype(BF16)
    csum = carry_ref[...]
    for part in _split_bf16(logf):
        csum = csum + jnp.dot(tri, part.astype(BF16), preferred_element_type=F32)
    carry_ref[...] = csum[TOKEN_TILE - 1:, :]
    lane = lax.broadcasted_iota(jnp.int32, (TOKEN_TILE, LANES), 1)
    slab = jnp.zeros((TOKEN_TILE, LANES), F32)
    for j, part in enumerate(_split_bf16(-csum)):
        slab = jnp.where((lane >= N_HEADS * j) & (lane < N_HEADS * (j + 1)), part, slab)
    c_ref[0] = slab.astype(BF16)

    gu = _gelu(u)
    gv = _gelu(vg)
    mu = jnp.mean(gv, axis=-1, keepdims=True)
    dv = gv - mu
    var = jnp.mean(dv * dv, axis=-1, keepdims=True)
    vn = ((dv * lax.rsqrt(var + LN_EPS)) * lng_ref[...] + lnb_ref[...]).astype(BF16)
    wrow = lax.broadcasted_iota(jnp.int32, (N_GROUPS * CHUNK, CHUNK), 0) % CHUNK
    wcol = lax.broadcasted_iota(jnp.int32, (N_GROUPS * CHUNK, CHUNK), 1)
    wsp = jnp.where(wcol <= wrow, wsp_ref[...], 0.0).astype(BF16)
    group = lax.broadcasted_iota(jnp.int32, (CHUNK, SGU_WIDTH), 1) // GROUP_DIM
    for c in range(TOKEN_TILE // CHUNK):
        rows = slice(c * CHUNK, (c + 1) * CHUNK)
        mixed_all = jnp.dot(wsp, vn[rows], preferred_element_type=F32)
        mixed = bsp_ref[...]
        for g in range(N_GROUPS):
            mixed = mixed + jnp.where(group == g, mixed_all[g * CHUNK:(g + 1) * CHUNK], 0.0)
        sgu_ref[rows, :] = (gu[rows] * mixed).astype(BF16)


def _proj(h, gain, wqt, wvt, wn, bf, lng, lnb, wsp, bsp, *, batch, seq):
    t = batch * seq
    tiles_per_seq = seq // TOKEN_TILE
    sub = TOKEN_TILE // ATTN_BLOCK
    tok = lambda i: (i, 0)
    seq_rows = lambda i: (i // tiles_per_seq, i % tiles_per_seq, 0)
    seq_cols = lambda i: (i // tiles_per_seq, 0, i % tiles_per_seq)
    return pl.pallas_call(
        functools.partial(_proj_kernel, tiles_per_seq=tiles_per_seq),
        grid=(t // TOKEN_TILE,),
        in_specs=[pl.BlockSpec((TOKEN_TILE, D_MODEL), tok), _resident((1, D_MODEL)),
                  _resident(wqt.shape), _resident(wvt.shape), _resident(wn.shape),
                  _resident((1, LANES)), _resident((1, SGU_WIDTH)), _resident((1, SGU_WIDTH)),
                  _resident(wsp.shape), _resident(bsp.shape)],
        out_specs=[pl.BlockSpec((1, ATTN_WIDTH, TOKEN_TILE), seq_cols),
                   pl.BlockSpec((1, TOKEN_TILE, ATTN_WIDTH), seq_rows),
                   pl.BlockSpec((1, TOKEN_TILE, LANES), seq_rows),
                   pl.BlockSpec((1, N_HEADS, sub, V_ROWS, ATTN_BLOCK),
                                lambda i: (i // tiles_per_seq, 0, i % tiles_per_seq, 0, 0)),
                   pl.BlockSpec((TOKEN_TILE, SGU_WIDTH), tok)],
        out_shape=[jax.ShapeDtypeStruct((batch, ATTN_WIDTH, seq), BF16),
                   jax.ShapeDtypeStruct((batch, seq, ATTN_WIDTH), BF16),
                   jax.ShapeDtypeStruct((batch, seq, LANES), BF16),
                   jax.ShapeDtypeStruct((batch, N_HEADS, seq // ATTN_BLOCK, V_ROWS, ATTN_BLOCK), BF16),
                   jax.ShapeDtypeStruct((t, SGU_WIDTH), BF16)],
        scratch_shapes=[pltpu.VMEM((1, LANES), F32)],
        compiler_params=_params(),
        name="proj",
    )(h, gain, wqt, wvt, wn, bf, lng, lnb, wsp, bsp)


def _attn_kernel(qt_ref, k_ref, c_ref, v_ref, o_ref, qaug_ref, m_ref, acc_ref):
    qi = pl.program_id(1)
    blk = ATTN_BLOCK

    sel_row = lax.broadcasted_iota(jnp.int32, (LANES, blk), 0)
    zero_half = jnp.zeros((HEAD_DIM, blk), BF16)
    for hd in range(N_HEADS):
        q_rows = qt_ref[0, hd * HEAD_DIM:(hd + 1) * HEAD_DIM, :]
        halves = (q_rows, zero_half) if hd % 2 == 0 else (zero_half, q_rows)
        qaug_ref[hd, :HEAD_DIM, :] = halves[0]
        qaug_ref[hd, HEAD_DIM:LANES, :] = halves[1]
        sel = sel_row == hd
        for j in range(1, N_SPLIT):
            sel = sel | (sel_row == hd + N_HEADS * j)
        qaug_ref[hd, LANES:, :] = jnp.where(sel, 1.0, 0.0).astype(BF16)

    m_ref[...] = jnp.full(m_ref.shape, -jnp.inf, F32)
    acc_ref[...] = jnp.zeros(acc_ref.shape, F32)

    key_pos = lax.broadcasted_iota(jnp.int32, (blk, blk), 0)
    qry_pos = lax.broadcasted_iota(jnp.int32, (blk, blk), 1)
    causal = key_pos <= qry_pos

    def block(j, diagonal):
        rows = pl.ds(pl.multiple_of(j * blk, blk), blk)
        cs = c_ref[0, rows, :]
        for pair in range(N_HEADS // 2):
            kx = jnp.concatenate([k_ref[0, rows, pair * LANES:(pair + 1) * LANES], cs], axis=1)
            for hd in (2 * pair, 2 * pair + 1):
                st = jnp.dot(kx, qaug_ref[hd], preferred_element_type=F32)
                if diagonal:
                    st = jnp.where(causal, st, -jnp.inf)
                m_old = m_ref[hd]
                m_new = jnp.maximum(m_old, jnp.max(st, axis=0, keepdims=True))
                alpha = jnp.exp(m_old - m_new)
                pt = jnp.exp(st - m_new).astype(BF16)
                pv = jnp.dot(v_ref[0, hd, j], pt, preferred_element_type=F32)
                acc_ref[hd] = acc_ref[hd] * alpha + pv
                m_ref[hd] = m_new

    def body(j, carry):
        block(j, diagonal=False)
        return carry

    lax.fori_loop(0, qi, body, 0)
    block(qi, diagonal=True)

    for hd in range(N_HEADS):
        acc = acc_ref[hd]
        o_ref[0, hd * HEAD_DIM:(hd + 1) * HEAD_DIM, :] = (
            acc[:HEAD_DIM] / acc[HEAD_DIM:HEAD_DIM + 1]).astype(BF16)


def _attn(qt, k, c, v, *, batch, seq):
    nblk = seq // ATTN_BLOCK
    return pl.pallas_call(
        _attn_kernel,
        grid=(batch, nblk),
        in_specs=[pl.BlockSpec((1, ATTN_WIDTH, ATTN_BLOCK), lambda b, i: (b, 0, i)),
                  pl.BlockSpec((1, seq, ATTN_WIDTH), lambda b, i: (b, 0, 0)),
                  pl.BlockSpec((1, seq, LANES), lambda b, i: (b, 0, 0)),
                  pl.BlockSpec((1, N_HEADS, nblk, V_ROWS, ATTN_BLOCK), lambda b, i: (b, 0, 0, 0, 0))],
        out_specs=pl.BlockSpec((1, ATTN_WIDTH, ATTN_BLOCK), lambda b, i: (b, 0, i)),
        out_shape=jax.ShapeDtypeStruct((batch, ATTN_WIDTH, seq), BF16),
        scratch_shapes=[pltpu.VMEM((N_HEADS, QK_DEPTH, ATTN_BLOCK), BF16),
                        pltpu.VMEM((N_HEADS, 1, ATTN_BLOCK), F32),
                        pltpu.VMEM((N_HEADS, V_ROWS, ATTN_BLOCK), F32)],
        compiler_params=pltpu.CompilerParams(dimension_semantics=("arbitrary", "arbitrary"),
                                             vmem_limit_bytes=VMEM_LIMIT),
        name="attn",
    )(qt, k, c, v)


def _merge_kernel(h_ref, at_ref, sgu_ref, gain_ref, wgate_ref, bgate_ref, wua_ref, wug_ref,
                  wout_ref, o_ref):
    h = h_ref[...]
    nb = _rms_norm(h, gain_ref[...]).astype(BF16)
    gates = jax.nn.sigmoid(jnp.dot(nb, wgate_ref[...], preferred_element_type=F32) + bgate_ref[...])
    up_a = lax.dot_general(at_ref[0], wua_ref[...], (((0,), (0,)), ((), ())),
                           preferred_element_type=F32)
    up_g = jnp.dot(sgu_ref[...], wug_ref[...], preferred_element_type=F32)
    merged = gates[:, :D_MODEL] * up_a + gates[:, D_MODEL:] * up_g
    o_ref[...] = h + jnp.dot(merged.astype(BF16), wout_ref[...], preferred_element_type=F32)


def _merge(h, at, sgu, gain, wgate, bgate, wua, wug, wout, *, seq):
    t = h.shape[0]
    tiles_per_seq = seq // TOKEN_TILE
    tok = lambda i: (i, 0)
    return pl.pallas_call(
        _merge_kernel,
        grid=(t // TOKEN_TILE,),
        in_specs=[pl.BlockSpec((TOKEN_TILE, D_MODEL), tok),
                  pl.BlockSpec((1, ATTN_WIDTH, TOKEN_TILE),
                               lambda i: (i // tiles_per_seq, 0, i % tiles_per_seq)),
                  pl.BlockSpec((TOKEN_TILE, SGU_WIDTH), tok),
                  _resident((1, D_MODEL)), _resident(wgate.shape), _resident(bgate.shape),
                  _resident(wua.shape), _resident(wug.shape), _resident(wout.shape)],
        out_specs=pl.BlockSpec((TOKEN_TILE, D_MODEL), tok),
        out_shape=jax.ShapeDtypeStruct((t, D_MODEL), F32),
        compiler_params=_params(),
        name="merge",
    )(h, at, sgu, gain, wgate, bgate, wua, wug, wout)


def kernel(x, ffn1_norm, ffn1_w_in, ffn1_w_out, mix_norm, w_in, b_forget, b_gate, sgu_ln_gain,
           sgu_ln_bias, sgu_w_spatial, sgu_b_spatial, w_up_attn, w_up_sgu, w_out, ffn2_norm,
           ffn2_w_in, ffn2_w_out, final_norm):
    batch, seq, _ = x.shape
    depth = ffn1_norm.shape[0]
    row = lambda v: v.reshape(1, -1)
    fgain = row(final_norm)
    h = x.reshape(batch * seq, D_MODEL)
    for l in range(depth):
        last = l == depth - 1
        h = _ffn(h, row(ffn1_norm[l]), ffn1_w_in[l][:, :D_FF].astype(BF16),
                 ffn1_w_in[l][:, D_FF:].astype(BF16), ffn1_w_out[l].astype(BF16), fgain,
                 final_norm=False)

        o_k, o_v, o_f = ATTN_WIDTH, 2 * ATTN_WIDTH, 3 * ATTN_WIDTH
        o_u = o_f + N_HEADS
        o_vg, o_g = o_u + SGU_WIDTH, o_u + 2 * SGU_WIDTH
        w = w_in[l]
        wqt = (w[:, :o_k].T * (HEAD_DIM ** -0.5)).astype(BF16)
        wvt = w[:, o_v:o_f].T.astype(BF16)
        wf = jnp.tile(w[:, o_f:o_u], (1, N_SPLIT))
        wf = jnp.pad(wf, ((0, 0), (0, LANES - N_SPLIT * N_HEADS)))
        wn = jnp.concatenate([w[:, o_k:o_v], w[:, o_u:o_g], wf], axis=1).astype(BF16)
        bf = jnp.pad(jnp.tile(b_forget[l], N_SPLIT), (0, LANES - N_SPLIT * N_HEADS)).reshape(1, LANES)
        wsp = sgu_w_spatial[l].reshape(N_GROUPS * CHUNK, CHUNK)
        bsp = jnp.repeat(sgu_b_spatial[l].T, GROUP_DIM, axis=1)

        qt, k, c, v, sgu = _proj(h, row(mix_norm[l]), wqt, wvt, wn, bf, row(sgu_ln_gain[l]),
                                 row(sgu_ln_bias[l]), wsp, bsp, batch=batch, seq=seq)
        at = _attn(qt, k, c, v, batch=batch, seq=seq)
        h = _merge(h, at, sgu, row(mix_norm[l]), w[:, o_g:].astype(BF16), row(b_gate[l]),
                   w_up_attn[l].astype(BF16), w_up_sgu[l].astype(BF16), w_out[l].astype(BF16),
                   seq=seq)
        h = _ffn(h, row(ffn2_norm[l]), ffn2_w_in[l][:, :D_FF].astype(BF16),
                 ffn2_w_in[l][:, D_FF:].astype(BF16), ffn2_w_out[l].astype(BF16), fgain,
                 final_norm=last)
    return h.reshape(batch, seq, D_MODEL)
```

```python
import functools

import jax
import jax.numpy as jnp
from jax import lax
from jax.experimental import pallas as pl
from jax.experimental.pallas import tpu as pltpu

F32 = jnp.float32
BF16 = jnp.bfloat16

D_MODEL = 1024
D_FF = 2816
N_HEADS = 8
HEAD_DIM = 64
ATTN_WIDTH = N_HEADS * HEAD_DIM
N_GROUPS = 8
GROUP_DIM = 64
SGU_WIDTH = N_GROUPS * GROUP_DIM
CHUNK = 128
RMS_EPS = 1e-6
LN_EPS = 1e-5

LANES = 128
V7X_VMEM_BYTES = 64 * 1024 * 1024
VMEM_LIMIT = V7X_VMEM_BYTES * 7 // 8

TOKEN_TILE = 512
FFN_COLS = 256
ATTN_BLOCK = 256
V_ROWS = HEAD_DIM + 16
N_SPLIT = 3
QK_DEPTH = 2 * LANES


def _rms_norm(x, gain):
    return (x * lax.rsqrt(jnp.mean(x * x, axis=-1, keepdims=True) + RMS_EPS)) * gain


def _gelu(x):
    return x * (lax.erf(x * (2.0 ** -0.5)) + 1.0) * 0.5


def _resident(shape):
    zeros = (0,) * len(shape)
    return pl.BlockSpec(shape, lambda *_: zeros, pipeline_mode=pl.Buffered(1))


def _params():
    return pltpu.CompilerParams(dimension_semantics=("arbitrary",),
                                vmem_limit_bytes=VMEM_LIMIT)


def _ffn_kernel(x_ref, gain_ref, wg_ref, wu_ref, wo_ref, fgain_ref, o_ref, act_ref, *, final_norm):
    x = x_ref[...]
    nb = _rms_norm(x, gain_ref[...]).astype(BF16)
    for j in range(D_FF // FFN_COLS):
        cols = slice(j * FFN_COLS, (j + 1) * FFN_COLS)
        gate = jnp.dot(nb, wg_ref[:, cols], preferred_element_type=F32)
        up = jnp.dot(nb, wu_ref[:, cols], preferred_element_type=F32)
        act_ref[:, cols] = (gate * jax.nn.sigmoid(gate) * up).astype(BF16)
    y = jnp.dot(act_ref[...], wo_ref[...], preferred_element_type=F32)
    h = x + 0.5 * y
    if final_norm:
        h = _rms_norm(h, fgain_ref[...])
    o_ref[...] = h


def _ffn(x, gain, wg, wu, wo, fgain, *, final_norm):
    t = x.shape[0]
    tile = pl.BlockSpec((TOKEN_TILE, D_MODEL), lambda i: (i, 0))
    return pl.pallas_call(
        functools.partial(_ffn_kernel, final_norm=final_norm),
        grid=(t // TOKEN_TILE,),
        in_specs=[tile, _resident((1, D_MODEL)), _resident((D_MODEL, D_FF)),
                  _resident((D_MODEL, D_FF)), _resident((D_FF, D_MODEL)),
                  _resident((1, D_MODEL))],
        out_specs=tile,
        out_shape=jax.ShapeDtypeStruct((t, D_MODEL), F32),
        scratch_shapes=[pltpu.VMEM((TOKEN_TILE, D_FF), BF16)],
        compiler_params=_params(),
        name="ffn_final" if final_norm else "ffn",
    )(x, gain, wg, wu, wo, fgain)


def _split_bf16(x):
    parts = []
    for _ in range(N_SPLIT):
        p = x.astype(BF16).astype(F32)
        parts.append(p)
        x = x - p
    return parts


def _proj_kernel(h_ref, gain_ref, wqt_ref, wvt_ref, wn_ref, bf_ref, lng_ref, lnb_ref,
                 wsp_ref, bsp_ref,
                 qt_ref, k_ref, c_ref, v_ref, sgu_ref, carry_ref, *, tiles_per_seq):
    i = pl.program_id(0)

    @pl.when(i % tiles_per_seq == 0)
    def _():
        carry_ref[...] = jnp.zeros_like(carry_ref)

    nb = _rms_norm(h_ref[...], gain_ref[...]).astype(BF16)

    tdims = (((1,), (1,)), ((), ()))
    qt = lax.dot_general(wqt_ref[...], nb, tdims, preferred_element_type=F32)
    qt_ref[0] = qt.astype(BF16)
    vt = lax.dot_general(wvt_ref[...], nb, tdims, preferred_element_type=F32).astype(BF16)
    pad_row = lax.broadcasted_iota(jnp.int32, (V_ROWS - HEAD_DIM, ATTN_BLOCK), 0)
    ones_row = jnp.where(pad_row == 0, 1.0, 0.0).astype(BF16)
    for s in range(TOKEN_TILE // ATTN_BLOCK):
        for hd in range(N_HEADS):
            v_ref[0, hd, s, :HEAD_DIM, :] = vt[hd * HEAD_DIM:(hd + 1) * HEAD_DIM,
                                               s * ATTN_BLOCK:(s + 1) * ATTN_BLOCK]
            v_ref[0, hd, s, HEAD_DIM:, :] = ones_row

    z = jnp.dot(nb, wn_ref[...], preferred_element_type=F32)
    k_ref[0] = z[:, :ATTN_WIDTH].astype(BF16)
    u = z[:, ATTN_WIDTH:ATTN_WIDTH + SGU_WIDTH]
    vg = z[:, ATTN_WIDTH + SGU_WIDTH:ATTN_WIDTH + 2 * SGU_WIDTH]
    f = z[:, ATTN_WIDTH + 2 * SGU_WIDTH:] + bf_ref[...]

    logf = jnp.minimum(f, 0.0) - jnp.log1p(jnp.exp(-jnp.abs(f)))
    row = lax.broadcasted_iota(jnp.int32, (TOKEN_TILE, TOKEN_TILE), 0)
    col = lax.broadcasted_iota(jnp.int32, (TOKEN_TILE, TOKEN_TILE), 1)
    tri = jnp.where(row >= col, 1.0, 0.0).astype(BF16)
    csum = carry_ref[...]
    for part in _split_bf16(logf):
        csum = csum + jnp.dot(tri, part.astype(BF16), preferred_element_type=F32)
    carry_ref[...] = csum[TOKEN_TILE - 1:, :]
    lane = lax.broadcasted_iota(jnp.int32, (TOKEN_TILE, LANES), 1)
    slab = jnp.zeros((TOKEN_TILE, LANES), F32)
    for j, part in enumerate(_split_bf16(-csum)):
        slab = jnp.where((lane >= N_HEADS * j) & (lane < N_HEADS * (j + 1)), part, slab)
    c_ref[0] = slab.astype(BF16)

    gu = _gelu(u)
    gv = _gelu(vg)
    mu = jnp.mean(gv, axis=-1, keepdims=True)
    dv = gv - mu
    var = jnp.mean(dv * dv, axis=-1, keepdims=True)
    vn = ((dv * lax.rsqrt(var + LN_EPS)) * lng_ref[...] + lnb_ref[...]).astype(BF16)
    wrow = lax.broadcasted_iota(jnp.int32, (N_GROUPS * CHUNK, CHUNK), 0) % CHUNK
    wcol = lax.broadcasted_iota(jnp.int32, (N_GROUPS * CHUNK, CHUNK), 1)
    wsp = jnp.where(wcol <= wrow, wsp_ref[...], 0.0).astype(BF16)
    group = lax.broadcasted_iota(jnp.int32, (CHUNK, SGU_WIDTH), 1) // GROUP_DIM
    for c in range(TOKEN_TILE // CHUNK):
        rows = slice(c * CHUNK, (c + 1) * CHUNK)
        mixed_all = jnp.dot(wsp, vn[rows], preferred_element_type=F32)
        mixed = bsp_ref[...]
        for g in range(N_GROUPS):
            mixed = mixed + jnp.where(group == g, mixed_all[g * CHUNK:(g + 1) * CHUNK], 0.0)
        sgu_ref[rows, :] = (gu[rows] * mixed).astype(BF16)


def _proj(h, gain, wqt, wvt, wn, bf, lng, lnb, wsp, bsp, *, batch, seq):
    t = batch * seq
    tiles_per_seq = seq // TOKEN_TILE
    sub = TOKEN_TILE // ATTN_BLOCK
    tok = lambda i: (i, 0)
    seq_rows = lambda i: (i // tiles_per_seq, i % tiles_per_seq, 0)
    seq_cols = lambda i: (i // tiles_per_seq, 0, i % tiles_per_seq)
    return pl.pallas_call(
        functools.partial(_proj_kernel, tiles_per_seq=tiles_per_seq),
        grid=(t // TOKEN_TILE,),
        in_specs=[pl.BlockSpec((TOKEN_TILE, D_MODEL), tok), _resident((1, D_MODEL)),
                  _resident(wqt.shape), _resident(wvt.shape), _resident(wn.shape),
                  _resident((1, LANES)), _resident((1, SGU_WIDTH)), _resident((1, SGU_WIDTH)),
                  _resident(wsp.shape), _resident(bsp.shape)],
        out_specs=[pl.BlockSpec((1, ATTN_WIDTH, TOKEN_TILE), seq_cols),
                   pl.BlockSpec((1, TOKEN_TILE, ATTN_WIDTH), seq_rows),
                   pl.BlockSpec((1, TOKEN_TILE, LANES), seq_rows),
                   pl.BlockSpec((1, N_HEADS, sub, V_ROWS, ATTN_BLOCK),
                                lambda i: (i // tiles_per_seq, 0, i % tiles_per_seq, 0, 0)),
                   pl.BlockSpec((TOKEN_TILE, SGU_WIDTH), tok)],
        out_shape=[jax.ShapeDtypeStruct((batch, ATTN_WIDTH, seq), BF16),
                   jax.ShapeDtypeStruct((batch, seq, ATTN_WIDTH), BF16),
                   jax.ShapeDtypeStruct((batch, seq, LANES), BF16),
                   jax.ShapeDtypeStruct((batch, N_HEADS, seq // ATTN_BLOCK, V_ROWS, ATTN_BLOCK), BF16),
                   jax.ShapeDtypeStruct((t, SGU_WIDTH), BF16)],
        scratch_shapes=[pltpu.VMEM((1, LANES), F32)],
        compiler_params=_params(),
        name="proj",
    )(h, gain, wqt, wvt, wn, bf, lng, lnb, wsp, bsp)


def _attn_kernel(qt_ref, k_ref, c_ref, v_ref, o_ref, qaug_ref, s_ref, m_ref, acc_ref):
    qi = pl.program_id(1)
    blk = ATTN_BLOCK

    sel_row = lax.broadcasted_iota(jnp.int32, (LANES, blk), 0)
    zero_half = jnp.zeros((HEAD_DIM, blk), BF16)
    for hd in range(N_HEADS):
        q_rows = qt_ref[0, hd * HEAD_DIM:(hd + 1) * HEAD_DIM, :]
        halves = (q_rows, zero_half) if hd % 2 == 0 else (zero_half, q_rows)
        qaug_ref[hd, :HEAD_DIM, :] = halves[0]
        qaug_ref[hd, HEAD_DIM:LANES, :] = halves[1]
        sel = sel_row == hd
        for j in range(1, N_SPLIT):
            sel = sel | (sel_row == hd + N_HEADS * j)
        qaug_ref[hd, LANES:, :] = jnp.where(sel, 1.0, 0.0).astype(BF16)

    m_ref[...] = jnp.full(m_ref.shape, -jnp.inf, F32)
    acc_ref[...] = jnp.zeros(acc_ref.shape, F32)

    key_pos = lax.broadcasted_iota(jnp.int32, (blk, blk), 0)
    qry_pos = lax.broadcasted_iota(jnp.int32, (blk, blk), 1)
    causal = key_pos <= qry_pos

    def scores(j, hd):
        rows = pl.ds(pl.multiple_of(j * blk, blk), blk)
        pair = hd // 2
        kx = jnp.concatenate([k_ref[0, rows, pair * LANES:(pair + 1) * LANES],
                              c_ref[0, rows, :]], axis=1)
        return jnp.dot(kx, qaug_ref[hd], preferred_element_type=F32)

    def step(j, diagonal):
        for hd in range(N_HEADS):
            st = s_ref[hd]
            if diagonal:
                st = jnp.where(causal, st, -jnp.inf)
            else:
                s_next = scores(j + 1, hd)
            m_old = m_ref[hd]
            m_new = jnp.maximum(m_old, jnp.max(st, axis=0, keepdims=True))
            alpha = jnp.exp(m_old - m_new)
            pt = jnp.exp(st - m_new).astype(BF16)
            pv = jnp.dot(v_ref[0, hd, j], pt, preferred_element_type=F32)
            acc_ref[hd] = acc_ref[hd] * alpha + pv
            m_ref[hd] = m_new
            if not diagonal:
                s_ref[hd] = s_next

    for hd in range(N_HEADS):
        s_ref[hd] = scores(0, hd)

    def body(j, carry):
        step(j, diagonal=False)
        return carry

    lax.fori_loop(0, qi, body, 0)
    step(qi, diagonal=True)

    for hd in range(N_HEADS):
        acc = acc_ref[hd]
        o_ref[0, hd * HEAD_DIM:(hd + 1) * HEAD_DIM, :] = (
            acc[:HEAD_DIM] / acc[HEAD_DIM:HEAD_DIM + 1]).astype(BF16)


def _attn(qt, k, c, v, *, batch, seq):
    nblk = seq // ATTN_BLOCK
    return pl.pallas_call(
        _attn_kernel,
        grid=(batch, nblk),
        in_specs=[pl.BlockSpec((1, ATTN_WIDTH, ATTN_BLOCK), lambda b, i: (b, 0, i)),
                  pl.BlockSpec((1, seq, ATTN_WIDTH), lambda b, i: (b, 0, 0)),
                  pl.BlockSpec((1, seq, LANES), lambda b, i: (b, 0, 0)),
                  pl.BlockSpec((1, N_HEADS, nblk, V_ROWS, ATTN_BLOCK), lambda b, i: (b, 0, 0, 0, 0))],
        out_specs=pl.BlockSpec((1, ATTN_WIDTH, ATTN_BLOCK), lambda b, i: (b, 0, i)),
        out_shape=jax.ShapeDtypeStruct((batch, ATTN_WIDTH, seq), BF16),
        scratch_shapes=[pltpu.VMEM((N_HEADS, QK_DEPTH, ATTN_BLOCK), BF16),
                        pltpu.VMEM((N_HEADS, ATTN_BLOCK, ATTN_BLOCK), F32),
                        pltpu.VMEM((N_HEADS, 1, ATTN_BLOCK), F32),
                        pltpu.VMEM((N_HEADS, V_ROWS, ATTN_BLOCK), F32)],
        compiler_params=pltpu.CompilerParams(dimension_semantics=("arbitrary", "arbitrary"),
                                             vmem_limit_bytes=VMEM_LIMIT),
        name="attn",
    )(qt, k, c, v)


def _merge_kernel(h_ref, at_ref, sgu_ref, gain_ref, wgate_ref, bgate_ref, wua_ref, wug_ref,
                  wout_ref, o_ref):
    h = h_ref[...]
    nb = _rms_norm(h, gain_ref[...]).astype(BF16)
    gates = jax.nn.sigmoid(jnp.dot(nb, wgate_ref[...], preferred_element_type=F32) + bgate_ref[...])
    up_a = lax.dot_general(at_ref[0], wua_ref[...], (((0,), (0,)), ((), ())),
                           preferred_element_type=F32)
    up_g = jnp.dot(sgu_ref[...], wug_ref[...], preferred_element_type=F32)
    merged = gates[:, :D_MODEL] * up_a + gates[:, D_MODEL:] * up_g
    o_ref[...] = h + jnp.dot(merged.astype(BF16), wout_ref[...], preferred_element_type=F32)


def _merge(h, at, sgu, gain, wgate, bgate, wua, wug, wout, *, seq):
    t = h.shape[0]
    tiles_per_seq = seq // TOKEN_TILE
    tok = lambda i: (i, 0)
    return pl.pallas_call(
        _merge_kernel,
        grid=(t // TOKEN_TILE,),
        in_specs=[pl.BlockSpec((TOKEN_TILE, D_MODEL), tok),
                  pl.BlockSpec((1, ATTN_WIDTH, TOKEN_TILE),
                               lambda i: (i // tiles_per_seq, 0, i % tiles_per_seq)),
                  pl.BlockSpec((TOKEN_TILE, SGU_WIDTH), tok),
                  _resident((1, D_MODEL)), _resident(wgate.shape), _resident(bgate.shape),
                  _resident(wua.shape), _resident(wug.shape), _resident(wout.shape)],
        out_specs=pl.BlockSpec((TOKEN_TILE, D_MODEL), tok),
        out_shape=jax.ShapeDtypeStruct((t, D_MODEL), F32),
        compiler_params=_params(),
        name="merge",
    )(h, at, sgu, gain, wgate, bgate, wua, wug, wout)


def kernel(x, ffn1_norm, ffn1_w_in, ffn1_w_out, mix_norm, w_in, b_forget, b_gate, sgu_ln_gain,
           sgu_ln_bias, sgu_w_spatial, sgu_b_spatial, w_up_attn, w_up_sgu, w_out, ffn2_norm,
           ffn2_w_in, ffn2_w_out, final_norm):
    batch, seq, _ = x.shape
    depth = ffn1_norm.shape[0]
    row = lambda v: v.reshape(1, -1)
    fgain = row(final_norm)
    h = x.reshape(batch * seq, D_MODEL)
    for l in range(depth):
        last = l == depth - 1
        h = _ffn(h, row(ffn1_norm[l]), ffn1_w_in[l][:, :D_FF].astype(BF16),
                 ffn1_w_in[l][:, D_FF:].astype(BF16), ffn1_w_out[l].astype(BF16), fgain,
                 final_norm=False)

        o_k, o_v, o_f = ATTN_WIDTH, 2 * ATTN_WIDTH, 3 * ATTN_WIDTH
        o_u = o_f + N_HEADS
        o_vg, o_g = o_u + SGU_WIDTH, o_u + 2 * SGU_WIDTH
        w = w_in[l]
        wqt = (w[:, :o_k].T * (HEAD_DIM ** -0.5)).astype(BF16)
        wvt = w[:, o_v:o_f].T.astype(BF16)
        wf = jnp.tile(w[:, o_f:o_u], (1, N_SPLIT))
        wf = jnp.pad(wf, ((0, 0), (0, LANES - N_SPLIT * N_HEADS)))
        wn = jnp.concatenate([w[:, o_k:o_v], w[:, o_u:o_g], wf], axis=1).astype(BF16)
        bf = jnp.pad(jnp.tile(b_forget[l], N_SPLIT), (0, LANES - N_SPLIT * N_HEADS)).reshape(1, LANES)
        wsp = sgu_w_spatial[l].reshape(N_GROUPS * CHUNK, CHUNK)
        bsp = jnp.repeat(sgu_b_spatial[l].T, GROUP_DIM, axis=1)

        qt, k, c, v, sgu = _proj(h, row(mix_norm[l]), wqt, wvt, wn, bf, row(sgu_ln_gain[l]),
                                 row(sgu_ln_bias[l]), wsp, bsp, batch=batch, seq=seq)
        at = _attn(qt, k, c, v, batch=batch, seq=seq)
        h = _merge(h, at, sgu, row(mix_norm[l]), w[:, o_g:].astype(BF16), row(b_gate[l]),
                   w_up_attn[l].astype(BF16), w_up_sgu[l].astype(BF16), w_out[l].astype(BF16),
                   seq=seq)
        h = _ffn(h, row(ffn2_norm[l]), ffn2_w_in[l][:, :D_FF].astype(BF16),
                 ffn2_w_in[l][:, D_FF:].astype(BF16), ffn2_w_out[l].astype(BF16), fgain,
                 final_norm=last)
    return h.reshape(batch, seq, D_MODEL)
```

```python
import functools

import jax
import jax.numpy as jnp
from jax import lax
from jax.experimental import pallas as pl
from jax.experimental.pallas import tpu as pltpu

F32 = jnp.float32
BF16 = jnp.bfloat16

D_MODEL = 1024
D_FF = 2816
N_HEADS = 8
HEAD_DIM = 64
ATTN_WIDTH = N_HEADS * HEAD_DIM
N_GROUPS = 8
GROUP_DIM = 64
SGU_WIDTH = N_GROUPS * GROUP_DIM
CHUNK = 128
RMS_EPS = 1e-6
LN_EPS = 1e-5
LOG2E = 1.4426950408889634

LANES = 128
V7X_VMEM_BYTES = 64 * 1024 * 1024
VMEM_LIMIT = V7X_VMEM_BYTES * 7 // 8

TOKEN_TILE = 512
FFN_COLS = 256
ATTN_BLOCK = 256
Q_TILE = 512
V_ROWS = HEAD_DIM + 16
N_SPLIT = 3
QK_DEPTH = 2 * LANES


def _rms_norm(x, gain):
    return (x * lax.rsqrt(jnp.mean(x * x, axis=-1, keepdims=True) + RMS_EPS)) * gain


def _gelu(x):
    return x * (lax.erf(x * (2.0 ** -0.5)) + 1.0) * 0.5


def _resident(shape):
    zeros = (0,) * len(shape)
    return pl.BlockSpec(shape, lambda *_: zeros, pipeline_mode=pl.Buffered(1))


def _params():
    return pltpu.CompilerParams(dimension_semantics=("arbitrary",),
                                vmem_limit_bytes=VMEM_LIMIT)


def _ffn_kernel(x_ref, gain_ref, wg_ref, wu_ref, wo_ref, fgain_ref, o_ref, act_ref, *, final_norm):
    x = x_ref[...]
    nb = _rms_norm(x, gain_ref[...]).astype(BF16)
    for j in range(D_FF // FFN_COLS):
        cols = slice(j * FFN_COLS, (j + 1) * FFN_COLS)
        gate = jnp.dot(nb, wg_ref[:, cols], preferred_element_type=F32)
        up = jnp.dot(nb, wu_ref[:, cols], preferred_element_type=F32)
        act_ref[:, cols] = (gate * jax.nn.sigmoid(gate) * up).astype(BF16)
    y = jnp.dot(act_ref[...], wo_ref[...], preferred_element_type=F32)
    h = x + 0.5 * y
    if final_norm:
        h = _rms_norm(h, fgain_ref[...])
    o_ref[...] = h


def _ffn(x, gain, wg, wu, wo, fgain, *, final_norm):
    t = x.shape[0]
    tile = pl.BlockSpec((TOKEN_TILE, D_MODEL), lambda i: (i, 0))
    return pl.pallas_call(
        functools.partial(_ffn_kernel, final_norm=final_norm),
        grid=(t // TOKEN_TILE,),
        in_specs=[tile, _resident((1, D_MODEL)), _resident((D_MODEL, D_FF)),
                  _resident((D_MODEL, D_FF)), _resident((D_FF, D_MODEL)),
                  _resident((1, D_MODEL))],
        out_specs=tile,
        out_shape=jax.ShapeDtypeStruct((t, D_MODEL), F32),
        scratch_shapes=[pltpu.VMEM((TOKEN_TILE, D_FF), BF16)],
        compiler_params=_params(),
        name="ffn_final" if final_norm else "ffn",
    )(x, gain, wg, wu, wo, fgain)


def _split_bf16(x):
    parts = []
    for _ in range(N_SPLIT):
        p = x.astype(BF16).astype(F32)
        parts.append(p)
        x = x - p
    return parts


def _proj_kernel(h_ref, gain_ref, wqt_ref, wvt_ref, wn_ref, bf_ref, lng_ref, lnb_ref,
                 wsp_ref, bsp_ref,
                 qt_ref, k_ref, c_ref, v_ref, sgu_ref, carry_ref, *, tiles_per_seq):
    i = pl.program_id(0)

    @pl.when(i % tiles_per_seq == 0)
    def _():
        carry_ref[...] = jnp.zeros_like(carry_ref)

    nb = _rms_norm(h_ref[...], gain_ref[...]).astype(BF16)

    tdims = (((1,), (1,)), ((), ()))
    qt = lax.dot_general(wqt_ref[...], nb, tdims, preferred_element_type=F32)
    qt_ref[0] = (qt * LOG2E).astype(BF16)
    vt = lax.dot_general(wvt_ref[...], nb, tdims, preferred_element_type=F32).astype(BF16)
    pad_row = lax.broadcasted_iota(jnp.int32, (V_ROWS - HEAD_DIM, ATTN_BLOCK), 0)
    ones_row = jnp.where(pad_row == 0, 1.0, 0.0).astype(BF16)
    for s in range(TOKEN_TILE // ATTN_BLOCK):
        for hd in range(N_HEADS):
            v_ref[0, hd, s, :HEAD_DIM, :] = vt[hd * HEAD_DIM:(hd + 1) * HEAD_DIM,
                                               s * ATTN_BLOCK:(s + 1) * ATTN_BLOCK]
            v_ref[0, hd, s, HEAD_DIM:, :] = ones_row

    z = jnp.dot(nb, wn_ref[...], preferred_element_type=F32)
    k_ref[0] = z[:, :ATTN_WIDTH].astype(BF16)
    u = z[:, ATTN_WIDTH:ATTN_WIDTH + SGU_WIDTH]
    vg = z[:, ATTN_WIDTH + SGU_WIDTH:ATTN_WIDTH + 2 * SGU_WIDTH]
    f = z[:, ATTN_WIDTH + 2 * SGU_WIDTH:] + bf_ref[...]

    logf = jnp.minimum(f, 0.0) - jnp.log1p(jnp.exp(-jnp.abs(f)))
    row = lax.broadcasted_iota(jnp.int32, (TOKEN_TILE, TOKEN_TILE), 0)
    col = lax.broadcasted_iota(jnp.int32, (TOKEN_TILE, TOKEN_TILE), 1)
    tri = jnp.where(row >= col, 1.0, 0.0).astype(BF16)
    csum = carry_ref[...]
    for part in _split_bf16(logf):
        csum = csum + jnp.dot(tri, part.astype(BF16), preferred_element_type=F32)
    carry_ref[...] = csum[TOKEN_TILE - 1:, :]
    lane = lax.broadcasted_iota(jnp.int32, (TOKEN_TILE, LANES), 1)
    slab = jnp.zeros((TOKEN_TILE, LANES), F32)
    for j, part in enumerate(_split_bf16(csum * -LOG2E)):
        slab = jnp.where((lane >= N_HEADS * j) & (lane < N_HEADS * (j + 1)), part, slab)
    c_ref[0] = slab.astype(BF16)

    gu = _gelu(u)
    gv = _gelu(vg)
    mu = jnp.mean(gv, axis=-1, keepdims=True)
    dv = gv - mu
    var = jnp.mean(dv * dv, axis=-1, keepdims=True)
    vn = ((dv * lax.rsqrt(var + LN_EPS)) * lng_ref[...] + lnb_ref[...]).astype(BF16)
    wrow = lax.broadcasted_iota(jnp.int32, (N_GROUPS * CHUNK, CHUNK), 0) % CHUNK
    wcol = lax.broadcasted_iota(jnp.int32, (N_GROUPS * CHUNK, CHUNK), 1)
    wsp = jnp.where(wcol <= wrow, wsp_ref[...], 0.0).astype(BF16)
    group = lax.broadcasted_iota(jnp.int32, (CHUNK, SGU_WIDTH), 1) // GROUP_DIM
    for c in range(TOKEN_TILE // CHUNK):
        rows = slice(c * CHUNK, (c + 1) * CHUNK)
        mixed_all = jnp.dot(wsp, vn[rows], preferred_element_type=F32)
        mixed = bsp_ref[...]
        for g in range(N_GROUPS):
            mixed = mixed + jnp.where(group == g, mixed_all[g * CHUNK:(g + 1) * CHUNK], 0.0)
        sgu_ref[rows, :] = (gu[rows] * mixed).astype(BF16)


def _proj(h, gain, wqt, wvt, wn, bf, lng, lnb, wsp, bsp, *, batch, seq):
    t = batch * seq
    tiles_per_seq = seq // TOKEN_TILE
    sub = TOKEN_TILE // ATTN_BLOCK
    tok = lambda i: (i, 0)
    seq_rows = lambda i: (i // tiles_per_seq, i % tiles_per_seq, 0)
    seq_cols = lambda i: (i // tiles_per_seq, 0, i % tiles_per_seq)
    return pl.pallas_call(
        functools.partial(_proj_kernel, tiles_per_seq=tiles_per_seq),
        grid=(t // TOKEN_TILE,),
        in_specs=[pl.BlockSpec((TOKEN_TILE, D_MODEL), tok), _resident((1, D_MODEL)),
                  _resident(wqt.shape), _resident(wvt.shape), _resident(wn.shape),
                  _resident((1, LANES)), _resident((1, SGU_WIDTH)), _resident((1, SGU_WIDTH)),
                  _resident(wsp.shape), _resident(bsp.shape)],
        out_specs=[pl.BlockSpec((1, ATTN_WIDTH, TOKEN_TILE), seq_cols),
                   pl.BlockSpec((1, TOKEN_TILE, ATTN_WIDTH), seq_rows),
                   pl.BlockSpec((1, TOKEN_TILE, LANES), seq_rows),
                   pl.BlockSpec((1, N_HEADS, sub, V_ROWS, ATTN_BLOCK),
                                lambda i: (i // tiles_per_seq, 0, i % tiles_per_seq, 0, 0)),
                   pl.BlockSpec((TOKEN_TILE, SGU_WIDTH), tok)],
        out_shape=[jax.ShapeDtypeStruct((batch, ATTN_WIDTH, seq), BF16),
                   jax.ShapeDtypeStruct((batch, seq, ATTN_WIDTH), BF16),
                   jax.ShapeDtypeStruct((batch, seq, LANES), BF16),
                   jax.ShapeDtypeStruct((batch, N_HEADS, seq // ATTN_BLOCK, V_ROWS, ATTN_BLOCK), BF16),
                   jax.ShapeDtypeStruct((t, SGU_WIDTH), BF16)],
        scratch_shapes=[pltpu.VMEM((1, LANES), F32)],
        compiler_params=_params(),
        name="proj",
    )(h, gain, wqt, wvt, wn, bf, lng, lnb, wsp, bsp)


def _attn_kernel(qt_ref, k_ref, c_ref, v_ref, o_ref, qaug_ref, s_ref, smax_ref, m_ref, acc_ref):
    blk = ATTN_BLOCK
    n_diag = Q_TILE // blk
    n_off = pl.program_id(1) * n_diag

    sel_row = lax.broadcasted_iota(jnp.int32, (LANES, Q_TILE), 0)
    zero_half = jnp.zeros((HEAD_DIM, Q_TILE), BF16)
    for hd in range(N_HEADS):
        q_rows = qt_ref[0, hd * HEAD_DIM:(hd + 1) * HEAD_DIM, :]
        halves = (q_rows, zero_half) if hd % 2 == 0 else (zero_half, q_rows)
        qaug_ref[hd, :HEAD_DIM, :] = halves[0]
        qaug_ref[hd, HEAD_DIM:LANES, :] = halves[1]
        sel = sel_row == hd
        for j in range(1, N_SPLIT):
            sel = sel | (sel_row == hd + N_HEADS * j)
        qaug_ref[hd, LANES:, :] = jnp.where(sel, 1.0, 0.0).astype(BF16)

    m_ref[...] = jnp.full(m_ref.shape, -jnp.inf, F32)
    acc_ref[...] = jnp.zeros(acc_ref.shape, F32)

    key_pos = lax.broadcasted_iota(jnp.int32, (blk, Q_TILE), 0)
    qry_pos = lax.broadcasted_iota(jnp.int32, (blk, Q_TILE), 1)

    def scores(j, hd, diag):
        rows = pl.ds(pl.multiple_of(j * blk, blk), blk)
        pair = hd // 2
        kx = jnp.concatenate([k_ref[0, rows, pair * LANES:(pair + 1) * LANES],
                              c_ref[0, rows, :]], axis=1)
        st = jnp.dot(kx, qaug_ref[hd], preferred_element_type=F32)
        if diag is not None:
            st = jnp.where(key_pos + diag * blk <= qry_pos, st, -jnp.inf)
        s_ref[hd] = st
        smax_ref[hd] = jnp.max(st, axis=0, keepdims=True)

    def step(j, j_next, next_diag=None):
        for hd in range(N_HEADS):
            st = s_ref[hd]
            m_old = m_ref[hd]
            m_new = jnp.maximum(m_old, smax_ref[hd])
            if j_next is not None:
                scores(j_next, hd, next_diag)
            alpha = jnp.exp2(m_old - m_new)
            pt = jnp.exp2(st - m_new).astype(BF16)
            pv = jnp.dot(v_ref[0, hd, j], pt, preferred_element_type=F32)
            acc_ref[hd] = acc_ref[hd] * alpha + pv
            m_ref[hd] = m_new

    for hd in range(N_HEADS):
        scores(n_off, hd, 0)
    for d in range(1, n_diag):
        step(n_off + d - 1, n_off + d, d)

    def body(j, carry):
        step(jnp.where(j == 0, n_off + n_diag - 1, j - 1), j)
        return carry

    lax.fori_loop(0, n_off, body, 0)
    step(jnp.where(n_off == 0, n_diag - 1, n_off - 1), None)

    for hd in range(N_HEADS):
        acc = acc_ref[hd]
        o_ref[0, hd * HEAD_DIM:(hd + 1) * HEAD_DIM, :] = (
            acc[:HEAD_DIM] / acc[HEAD_DIM:HEAD_DIM + 1]).astype(BF16)


def _attn(qt, k, c, v, *, batch, seq):
    nblk = seq // ATTN_BLOCK
    return pl.pallas_call(
        _attn_kernel,
        grid=(batch, seq // Q_TILE),
        in_specs=[pl.BlockSpec((1, ATTN_WIDTH, Q_TILE), lambda b, i: (b, 0, i)),
                  pl.BlockSpec((1, seq, ATTN_WIDTH), lambda b, i: (b, 0, 0)),
                  pl.BlockSpec((1, seq, LANES), lambda b, i: (b, 0, 0)),
                  pl.BlockSpec((1, N_HEADS, nblk, V_ROWS, ATTN_BLOCK), lambda b, i: (b, 0, 0, 0, 0))],
        out_specs=pl.BlockSpec((1, ATTN_WIDTH, Q_TILE), lambda b, i: (b, 0, i)),
        out_shape=jax.ShapeDtypeStruct((batch, ATTN_WIDTH, seq), BF16),
        scratch_shapes=[pltpu.VMEM((N_HEADS, QK_DEPTH, Q_TILE), BF16),
                        pltpu.VMEM((N_HEADS, ATTN_BLOCK, Q_TILE), F32),
                        pltpu.VMEM((N_HEADS, 1, Q_TILE), F32),
                        pltpu.VMEM((N_HEADS, 1, Q_TILE), F32),
                        pltpu.VMEM((N_HEADS, V_ROWS, Q_TILE), F32)],
        compiler_params=pltpu.CompilerParams(dimension_semantics=("arbitrary", "arbitrary"),
                                             vmem_limit_bytes=VMEM_LIMIT),
        name="attn",
    )(qt, k, c, v)


def _merge_kernel(h_ref, at_ref, sgu_ref, gain_ref, wgate_ref, bgate_ref, wua_ref, wug_ref,
                  wout_ref, o_ref):
    h = h_ref[...]
    nb = _rms_norm(h, gain_ref[...]).astype(BF16)
    gates = jax.nn.sigmoid(jnp.dot(nb, wgate_ref[...], preferred_element_type=F32) + bgate_ref[...])
    up_a = lax.dot_general(at_ref[0], wua_ref[...], (((0,), (0,)), ((), ())),
                           preferred_element_type=F32)
    up_g = jnp.dot(sgu_ref[...], wug_ref[...], preferred_element_type=F32)
    merged = gates[:, :D_MODEL] * up_a + gates[:, D_MODEL:] * up_g
    o_ref[...] = h + jnp.dot(merged.astype(BF16), wout_ref[...], preferred_element_type=F32)


def _merge(h, at, sgu, gain, wgate, bgate, wua, wug, wout, *, seq):
    t = h.shape[0]
    tiles_per_seq = seq // TOKEN_TILE
    tok = lambda i: (i, 0)
    return pl.pallas_call(
        _merge_kernel,
        grid=(t // TOKEN_TILE,),
        in_specs=[pl.BlockSpec((TOKEN_TILE, D_MODEL), tok),
                  pl.BlockSpec((1, ATTN_WIDTH, TOKEN_TILE),
                               lambda i: (i // tiles_per_seq, 0, i % tiles_per_seq)),
                  pl.BlockSpec((TOKEN_TILE, SGU_WIDTH), tok),
                  _resident((1, D_MODEL)), _resident(wgate.shape), _resident(bgate.shape),
                  _resident(wua.shape), _resident(wug.shape), _resident(wout.shape)],
        out_specs=pl.BlockSpec((TOKEN_TILE, D_MODEL), tok),
        out_shape=jax.ShapeDtypeStruct((t, D_MODEL), F32),
        compiler_params=_params(),
        name="merge",
    )(h, at, sgu, gain, wgate, bgate, wua, wug, wout)


def kernel(x, ffn1_norm, ffn1_w_in, ffn1_w_out, mix_norm, w_in, b_forget, b_gate, sgu_ln_gain,
           sgu_ln_bias, sgu_w_spatial, sgu_b_spatial, w_up_attn, w_up_sgu, w_out, ffn2_norm,
           ffn2_w_in, ffn2_w_out, final_norm):
    batch, seq, _ = x.shape
    depth = ffn1_norm.shape[0]
    row = lambda v: v.reshape(1, -1)
    fgain = row(final_norm)
    h = x.reshape(batch * seq, D_MODEL)
    for l in range(depth):
        last = l == depth - 1
        h = _ffn(h, row(ffn1_norm[l]), ffn1_w_in[l][:, :D_FF].astype(BF16),
                 ffn1_w_in[l][:, D_FF:].astype(BF16), ffn1_w_out[l].astype(BF16), fgain,
                 final_norm=False)

        o_k, o_v, o_f = ATTN_WIDTH, 2 * ATTN_WIDTH, 3 * ATTN_WIDTH
        o_u = o_f + N_HEADS
        o_vg, o_g = o_u + SGU_WIDTH, o_u + 2 * SGU_WIDTH
        w = w_in[l]
        wqt = (w[:, :o_k].T * (HEAD_DIM ** -0.5)).astype(BF16)
        wvt = w[:, o_v:o_f].T.astype(BF16)
        wf = jnp.tile(w[:, o_f:o_u], (1, N_SPLIT))
        wf = jnp.pad(wf, ((0, 0), (0, LANES - N_SPLIT * N_HEADS)))
        wn = jnp.concatenate([w[:, o_k:o_v], w[:, o_u:o_g], wf], axis=1).astype(BF16)
        bf = jnp.pad(jnp.tile(b_forget[l], N_SPLIT), (0, LANES - N_SPLIT * N_HEADS)).reshape(1, LANES)
        wsp = sgu_w_spatial[l].reshape(N_GROUPS * CHUNK, CHUNK)
        bsp = jnp.repeat(sgu_b_spatial[l].T, GROUP_DIM, axis=1)

        qt, k, c, v, sgu = _proj(h, row(mix_norm[l]), wqt, wvt, wn, bf, row(sgu_ln_gain[l]),
                                 row(sgu_ln_bias[l]), wsp, bsp, batch=batch, seq=seq)
        at = _attn(qt, k, c, v, batch=batch, seq=seq)
        h = _merge(h, at, sgu, row(mix_norm[l]), w[:, o_g:].astype(BF16), row(b_gate[l]),
                   w_up_attn[l].astype(BF16), w_up_sgu[l].astype(BF16), w_out[l].astype(BF16),
                   seq=seq)
        h = _ffn(h, row(ffn2_norm[l]), ffn2_w_in[l][:, :D_FF].astype(BF16),
                 ffn2_w_in[l][:, D_FF:].astype(BF16), ffn2_w_out[l].astype(BF16), fgain,
                 final_norm=last)
    return h.reshape(batch, seq, D_MODEL)
```

```python
import functools

import jax
import jax.numpy as jnp
from jax import lax
from jax.experimental import pallas as pl
from jax.experimental.pallas import tpu as pltpu

F32 = jnp.float32
BF16 = jnp.bfloat16

D_MODEL = 1024
D_FF = 2816
N_HEADS = 8
HEAD_DIM = 64
ATTN_WIDTH = N_HEADS * HEAD_DIM
N_GROUPS = 8
GROUP_DIM = 64
SGU_WIDTH = N_GROUPS * GROUP_DIM
CHUNK = 128
RMS_EPS = 1e-6
LN_EPS = 1e-5
LOG2E = 1.4426950408889634

LANES = 128
V7X_VMEM_BYTES = 64 * 1024 * 1024
VMEM_LIMIT = V7X_VMEM_BYTES * 7 // 8

TOKEN_TILE = 512
FFN_COLS = 256
ATTN_BLOCK = 256
Q_TILE = 512
V_ROWS = HEAD_DIM + 16
N_SPLIT = 3
QK_DEPTH = LANES
SGU_PACK = 4


def _rms_norm(x, gain):
    return (x * lax.rsqrt(jnp.mean(x * x, axis=-1, keepdims=True) + RMS_EPS)) * gain


def _gelu(x):
    return x * (lax.erf(x * (2.0 ** -0.5)) + 1.0) * 0.5


def _resident(shape):
    zeros = (0,) * len(shape)
    return pl.BlockSpec(shape, lambda *_: zeros, pipeline_mode=pl.Buffered(1))


def _params():
    return pltpu.CompilerParams(dimension_semantics=("arbitrary",),
                                vmem_limit_bytes=VMEM_LIMIT)


def _ffn_kernel(x_ref, gain_ref, wg_ref, wu_ref, wo_ref, fgain_ref, o_ref, act_ref, *, final_norm):
    x = x_ref[...]
    nb = _rms_norm(x, gain_ref[...]).astype(BF16)
    for j in range(D_FF // FFN_COLS):
        cols = slice(j * FFN_COLS, (j + 1) * FFN_COLS)
        gate = jnp.dot(nb, wg_ref[:, cols], preferred_element_type=F32)
        up = jnp.dot(nb, wu_ref[:, cols], preferred_element_type=F32)
        act_ref[:, cols] = (gate * jax.nn.sigmoid(gate) * up).astype(BF16)
    y = jnp.dot(act_ref[...], wo_ref[...], preferred_element_type=F32)
    h = x + 0.5 * y
    if final_norm:
        h = _rms_norm(h, fgain_ref[...])
    o_ref[...] = h


def _ffn(x, gain, wg, wu, wo, fgain, *, final_norm):
    t = x.shape[0]
    tile = pl.BlockSpec((TOKEN_TILE, D_MODEL), lambda i: (i, 0))
    return pl.pallas_call(
        functools.partial(_ffn_kernel, final_norm=final_norm),
        grid=(t // TOKEN_TILE,),
        in_specs=[tile, _resident((1, D_MODEL)), _resident((D_MODEL, D_FF)),
                  _resident((D_MODEL, D_FF)), _resident((D_FF, D_MODEL)),
                  _resident((1, D_MODEL))],
        out_specs=tile,
        out_shape=jax.ShapeDtypeStruct((t, D_MODEL), F32),
        scratch_shapes=[pltpu.VMEM((TOKEN_TILE, D_FF), BF16)],
        compiler_params=_params(),
        name="ffn_final" if final_norm else "ffn",
    )(x, gain, wg, wu, wo, fgain)


def _split_bf16(x):
    parts = []
    for _ in range(N_SPLIT):
        p = x.astype(BF16).astype(F32)
        parts.append(p)
        x = x - p
    return parts


def _proj_kernel(h_ref, gain_ref, wqt_ref, wvt_ref, wn_ref, bf_ref, lng_ref, lnb_ref,
                 wsp_ref, bsp_ref,
                 q_ref, kx_ref, v_ref, sgu_ref, carry_ref, *, tiles_per_seq):
    i = pl.program_id(0)

    @pl.when(i % tiles_per_seq == 0)
    def _():
        carry_ref[...] = jnp.zeros_like(carry_ref)

    nb = _rms_norm(h_ref[...], gain_ref[...]).astype(BF16)

    tdims = (((1,), (1,)), ((), ()))
    qt = lax.dot_general(wqt_ref[...], nb, tdims, preferred_element_type=F32)
    qt = (qt * LOG2E).astype(BF16)
    sel_row = lax.broadcasted_iota(jnp.int32, (QK_DEPTH - HEAD_DIM, TOKEN_TILE), 0)
    ones_rows = jnp.where(sel_row < N_SPLIT, 1.0, 0.0).astype(BF16)
    for hd in range(N_HEADS):
        q_ref[0, hd, :HEAD_DIM, :] = qt[hd * HEAD_DIM:(hd + 1) * HEAD_DIM, :]
        q_ref[0, hd, HEAD_DIM:, :] = ones_rows
    vt = lax.dot_general(wvt_ref[...], nb, tdims, preferred_element_type=F32).astype(BF16)
    pad_row = lax.broadcasted_iota(jnp.int32, (V_ROWS - HEAD_DIM, ATTN_BLOCK), 0)
    ones_row = jnp.where(pad_row == 0, 1.0, 0.0).astype(BF16)
    for s in range(TOKEN_TILE // ATTN_BLOCK):
        for hd in range(N_HEADS):
            v_ref[0, hd, s, :HEAD_DIM, :] = vt[hd * HEAD_DIM:(hd + 1) * HEAD_DIM,
                                               s * ATTN_BLOCK:(s + 1) * ATTN_BLOCK]
            v_ref[0, hd, s, HEAD_DIM:, :] = ones_row

    z = jnp.dot(nb, wn_ref[...], preferred_element_type=F32)
    zk = z[:, :ATTN_WIDTH]
    u = z[:, ATTN_WIDTH:ATTN_WIDTH + SGU_WIDTH]
    vg = z[:, ATTN_WIDTH + SGU_WIDTH:ATTN_WIDTH + 2 * SGU_WIDTH]
    f = z[:, ATTN_WIDTH + 2 * SGU_WIDTH:] + bf_ref[...]
    logf = jnp.minimum(f, 0.0) - jnp.log1p(jnp.exp(-jnp.abs(f)))

    lane = lax.broadcasted_iota(jnp.int32, (CHUNK, LANES), 1)
    term = lane % N_SPLIT
    live = lane < N_SPLIT * N_HEADS

    def by_term(parts):
        out = parts[-1]
        for j in range(N_SPLIT - 2, -1, -1):
            out = jnp.where(term == j, parts[j], out)
        return out

    row = lax.broadcasted_iota(jnp.int32, (CHUNK, CHUNK), 0)
    col = lax.broadcasted_iota(jnp.int32, (CHUNK, CHUNK), 1)
    tri = jnp.where(row >= col, 1.0, 0.0).astype(BF16)
    carry = carry_ref[...]
    for c in range(TOKEN_TILE // CHUNK):
        rows = slice(c * CHUNK, (c + 1) * CHUNK)
        local = jnp.dot(tri, by_term(_split_bf16(logf[rows])).astype(BF16),
                        preferred_element_type=F32)
        shifted = {0: local}
        for d in range(1, N_SPLIT):
            shifted[d] = pltpu.roll(local, d, 1)
            shifted[-d] = pltpu.roll(local, LANES - d, 1)
        csum = carry
        for j in range(N_SPLIT):
            csum = csum + by_term([shifted[t - j] for t in range(N_SPLIT)])
        carry = csum[CHUNK - 1:, :]
        bias = jnp.where(live, by_term(_split_bf16(csum * -LOG2E)), 0.0)
        for hd in range(N_HEADS):
            pair = hd // 2
            kslab = zk[rows, pair * LANES:(pair + 1) * LANES]
            if hd % 2:
                kslab = pltpu.roll(kslab, HEAD_DIM, 1)
            hbias = pltpu.roll(bias, HEAD_DIM - N_SPLIT * hd, 1)
            kx = jnp.where(lane < HEAD_DIM, kslab,
                           jnp.where(lane < HEAD_DIM + N_SPLIT, hbias, 0.0))
            kx_ref[0, hd, rows, :] = kx.astype(BF16)
    carry_ref[...] = carry

    gu = _gelu(u)
    gv = _gelu(vg)
    mu = jnp.mean(gv, axis=-1, keepdims=True)
    dv = gv - mu
    var = jnp.mean(dv * dv, axis=-1, keepdims=True)
    vn = ((dv * lax.rsqrt(var + LN_EPS)) * lng_ref[...] + lnb_ref[...]).astype(BF16)
    pack_w = SGU_PACK * GROUP_DIM
    wrow = lax.broadcasted_iota(jnp.int32, (CHUNK, SGU_PACK * CHUNK), 0)
    wcol = lax.broadcasted_iota(jnp.int32, (CHUNK, SGU_PACK * CHUNK), 1) % CHUNK
    wsp = [jnp.where(wcol <= wrow, wsp_ref[p], 0.0).astype(BF16)
           for p in range(N_GROUPS // SGU_PACK)]
    lane_group = lax.broadcasted_iota(jnp.int32, (CHUNK, pack_w), 1) // GROUP_DIM
    for c in range(TOKEN_TILE // CHUNK):
        rows = slice(c * CHUNK, (c + 1) * CHUNK)
        mixed = []
        for p in range(N_GROUPS // SGU_PACK):
            slab = vn[rows, p * pack_w:(p + 1) * pack_w]
            blocks = jnp.concatenate(
                [jnp.where(lane_group == r, slab, jnp.zeros_like(slab)) for r in range(SGU_PACK)],
                axis=0)
            mixed.append(jnp.dot(wsp[p], blocks, preferred_element_type=F32))
        mixed = jnp.concatenate(mixed, axis=1) + bsp_ref[...]
        sgu_ref[rows, :] = (gu[rows] * mixed).astype(BF16)


def _proj(h, gain, wqt, wvt, wn, bf, lng, lnb, wsp, bsp, *, batch, seq):
    t = batch * seq
    tiles_per_seq = seq // TOKEN_TILE
    sub = TOKEN_TILE // ATTN_BLOCK
    tok = lambda i: (i, 0)
    return pl.pallas_call(
        functools.partial(_proj_kernel, tiles_per_seq=tiles_per_seq),
        grid=(t // TOKEN_TILE,),
        in_specs=[pl.BlockSpec((TOKEN_TILE, D_MODEL), tok), _resident((1, D_MODEL)),
                  _resident(wqt.shape), _resident(wvt.shape), _resident(wn.shape),
                  _resident((1, LANES)), _resident((1, SGU_WIDTH)), _resident((1, SGU_WIDTH)),
                  _resident(wsp.shape), _resident(bsp.shape)],
        out_specs=[pl.BlockSpec((1, N_HEADS, QK_DEPTH, TOKEN_TILE),
                                lambda i: (i // tiles_per_seq, 0, 0, i % tiles_per_seq)),
                   pl.BlockSpec((1, N_HEADS, TOKEN_TILE, LANES),
                                lambda i: (i // tiles_per_seq, 0, i % tiles_per_seq, 0)),
                   pl.BlockSpec((1, N_HEADS, sub, V_ROWS, ATTN_BLOCK),
                                lambda i: (i // tiles_per_seq, 0, i % tiles_per_seq, 0, 0)),
                   pl.BlockSpec((TOKEN_TILE, SGU_WIDTH), tok)],
        out_shape=[jax.ShapeDtypeStruct((batch, N_HEADS, QK_DEPTH, seq), BF16),
                   jax.ShapeDtypeStruct((batch, N_HEADS, seq, LANES), BF16),
                   jax.ShapeDtypeStruct((batch, N_HEADS, seq // ATTN_BLOCK, V_ROWS, ATTN_BLOCK), BF16),
                   jax.ShapeDtypeStruct((t, SGU_WIDTH), BF16)],
        scratch_shapes=[pltpu.VMEM((1, LANES), F32)],
        compiler_params=_params(),
        name="proj",
    )(h, gain, wqt, wvt, wn, bf, lng, lnb, wsp, bsp)


def _attn_kernel(q_ref, kx_ref, v_ref, o_ref, s_ref, smax_ref, m_ref, acc_ref):
    blk = ATTN_BLOCK
    n_diag = Q_TILE // blk
    n_off = pl.program_id(1) * n_diag

    m_ref[...] = jnp.full(m_ref.shape, -jnp.inf, F32)
    acc_ref[...] = jnp.zeros(acc_ref.shape, F32)

    key_pos = lax.broadcasted_iota(jnp.int32, (blk, Q_TILE), 0)
    qry_pos = lax.broadcasted_iota(jnp.int32, (blk, Q_TILE), 1)
    causal = key_pos <= qry_pos

    def scores(j, hd, diag):
        rows = pl.ds(pl.multiple_of(j * blk, blk), blk)
        lo = 0 if diag is None else diag * blk
        st = jnp.dot(kx_ref[0, hd, rows, :], q_ref[0, hd, :, lo:], preferred_element_type=F32)
        if diag is not None:
            st = jnp.where(causal[:, :Q_TILE - lo], st, -jnp.inf)
        if lo:
            s_ref[hd, :, :lo] = jnp.full((blk, lo), -jnp.inf, F32)
            smax_ref[hd, :, :lo] = jnp.full((1, lo), -jnp.inf, F32)
        s_ref[hd, :, lo:] = st
        smax_ref[hd, :, lo:] = jnp.max(st, axis=0, keepdims=True)

    def step(j, j_next, next_diag=None):
        for hd in range(N_HEADS):
            st = s_ref[hd]
            m_old = m_ref[hd]
            m_new = jnp.maximum(m_old, smax_ref[hd])
            if j_next is not None:
                scores(j_next, hd, next_diag)
            alpha = jnp.exp2(m_old - m_new)
            pt = jnp.exp2(st - m_new).astype(BF16)
            pv = jnp.dot(v_ref[0, hd, j], pt, preferred_element_type=F32)
            acc_ref[hd] = acc_ref[hd] * alpha + pv
            m_ref[hd] = m_new

    for hd in range(N_HEADS):
        scores(n_off, hd, 0)
    for d in range(1, n_diag):
        step(n_off + d - 1, n_off + d, d)

    def body(i, carry):
        j0 = i * n_diag
        step(jnp.where(i == 0, n_off + n_diag - 1, j0 - 1), j0)
        for d in range(1, n_diag):
            step(j0 + d - 1, j0 + d)
        return carry

    lax.fori_loop(0, pl.program_id(1), body, 0)
    step(jnp.where(n_off == 0, n_diag - 1, n_off - 1), None)

    for hd in range(N_HEADS):
        acc = acc_ref[hd]
        o_ref[0, hd * HEAD_DIM:(hd + 1) * HEAD_DIM, :] = (
            acc[:HEAD_DIM] / acc[HEAD_DIM:HEAD_DIM + 1]).astype(BF16)


def _attn(q, kx, v, *, batch, seq):
    nblk = seq // ATTN_BLOCK
    return pl.pallas_call(
        _attn_kernel,
        grid=(batch, seq // Q_TILE),
        in_specs=[pl.BlockSpec((1, N_HEADS, QK_DEPTH, Q_TILE), lambda b, i: (b, 0, 0, i)),
                  pl.BlockSpec((1, N_HEADS, seq, LANES), lambda b, i: (b, 0, 0, 0)),
                  pl.BlockSpec((1, N_HEADS, nblk, V_ROWS, ATTN_BLOCK), lambda b, i: (b, 0, 0, 0, 0))],
        out_specs=pl.BlockSpec((1, ATTN_WIDTH, Q_TILE), lambda b, i: (b, 0, i)),
        out_shape=jax.ShapeDtypeStruct((batch, ATTN_WIDTH, seq), BF16),
        scratch_shapes=[pltpu.VMEM((N_HEADS, ATTN_BLOCK, Q_TILE), F32),
                        pltpu.VMEM((N_HEADS, 1, Q_TILE), F32),
                        pltpu.VMEM((N_HEADS, 1, Q_TILE), F32),
                        pltpu.VMEM((N_HEADS, V_ROWS, Q_TILE), F32)],
        compiler_params=pltpu.CompilerParams(dimension_semantics=("arbitrary", "arbitrary"),
                                             vmem_limit_bytes=VMEM_LIMIT),
        name="attn",
    )(q, kx, v)


def _merge_kernel(h_ref, at_ref, sgu_ref, gain_ref, wgate_ref, bgate_ref, wua_ref, wug_ref,
                  wout_ref, o_ref):
    h = h_ref[...]
    nb = _rms_norm(h, gain_ref[...]).astype(BF16)
    gates = jax.nn.sigmoid(jnp.dot(nb, wgate_ref[...], preferred_element_type=F32) + bgate_ref[...])
    up_a = lax.dot_general(at_ref[0], wua_ref[...], (((0,), (0,)), ((), ())),
                           preferred_element_type=F32)
    up_g = jnp.dot(sgu_ref[...], wug_ref[...], preferred_element_type=F32)
    merged = gates[:, :D_MODEL] * up_a + gates[:, D_MODEL:] * up_g
    o_ref[...] = h + jnp.dot(merged.astype(BF16), wout_ref[...], preferred_element_type=F32)


def _merge(h, at, sgu, gain, wgate, bgate, wua, wug, wout, *, seq):
    t = h.shape[0]
    tiles_per_seq = seq // TOKEN_TILE
    tok = lambda i: (i, 0)
    return pl.pallas_call(
        _merge_kernel,
        grid=(t // TOKEN_TILE,),
        in_specs=[pl.BlockSpec((TOKEN_TILE, D_MODEL), tok),
                  pl.BlockSpec((1, ATTN_WIDTH, TOKEN_TILE),
                               lambda i: (i // tiles_per_seq, 0, i % tiles_per_seq)),
                  pl.BlockSpec((TOKEN_TILE, SGU_WIDTH), tok),
                  _resident((1, D_MODEL)), _resident(wgate.shape), _resident(bgate.shape),
                  _resident(wua.shape), _resident(wug.shape), _resident(wout.shape)],
        out_specs=pl.BlockSpec((TOKEN_TILE, D_MODEL), tok),
        out_shape=jax.ShapeDtypeStruct((t, D_MODEL), F32),
        compiler_params=_params(),
        name="merge",
    )(h, at, sgu, gain, wgate, bgate, wua, wug, wout)


def kernel(x, ffn1_norm, ffn1_w_in, ffn1_w_out, mix_norm, w_in, b_forget, b_gate, sgu_ln_gain,
           sgu_ln_bias, sgu_w_spatial, sgu_b_spatial, w_up_attn, w_up_sgu, w_out, ffn2_norm,
           ffn2_w_in, ffn2_w_out, final_norm):
    batch, seq, _ = x.shape
    depth = ffn1_norm.shape[0]
    row = lambda v: v.reshape(1, -1)
    fgain = row(final_norm)
    h = x.reshape(batch * seq, D_MODEL)
    for l in range(depth):
        last = l == depth - 1
        h = _ffn(h, row(ffn1_norm[l]), ffn1_w_in[l][:, :D_FF].astype(BF16),
                 ffn1_w_in[l][:, D_FF:].astype(BF16), ffn1_w_out[l].astype(BF16), fgain,
                 final_norm=False)

        o_k, o_v, o_f = ATTN_WIDTH, 2 * ATTN_WIDTH, 3 * ATTN_WIDTH
        o_u = o_f + N_HEADS
        o_vg, o_g = o_u + SGU_WIDTH, o_u + 2 * SGU_WIDTH
        w = w_in[l]
        wqt = (w[:, :o_k].T * (HEAD_DIM ** -0.5)).astype(BF16)
        wvt = w[:, o_v:o_f].T.astype(BF16)
        wf = jnp.repeat(w[:, o_f:o_u], N_SPLIT, axis=1)
        wf = jnp.pad(wf, ((0, 0), (0, LANES - N_SPLIT * N_HEADS)))
        wn = jnp.concatenate([w[:, o_k:o_v], w[:, o_u:o_g], wf], axis=1).astype(BF16)
        bf = jnp.pad(jnp.repeat(b_forget[l], N_SPLIT), (0, LANES - N_SPLIT * N_HEADS)).reshape(1, LANES)
        wsp = sgu_w_spatial[l].reshape(N_GROUPS // SGU_PACK, SGU_PACK, CHUNK, CHUNK)
        wsp = wsp.transpose(0, 2, 1, 3).reshape(N_GROUPS // SGU_PACK, CHUNK, SGU_PACK * CHUNK)
        bsp = jnp.repeat(sgu_b_spatial[l].T, GROUP_DIM, axis=1)

        q, kx, v, sgu = _proj(h, row(mix_norm[l]), wqt, wvt, wn, bf, row(sgu_ln_gain[l]),
                              row(sgu_ln_bias[l]), wsp, bsp, batch=batch, seq=seq)
        at = _attn(q, kx, v, batch=batch, seq=seq)
        h = _merge(h, at, sgu, row(mix_norm[l]), w[:, o_g:].astype(BF16), row(b_gate[l]),
                   w_up_attn[l].astype(BF16), w_up_sgu[l].astype(BF16), w_out[l].astype(BF16),
                   seq=seq)
        h = _ffn(h, row(ffn2_norm[l]), ffn2_w_in[l][:, :D_FF].astype(BF16),
                 ffn2_w_in[l][:, D_FF:].astype(BF16), ffn2_w_out[l].astype(BF16), fgain,
                 final_norm=last)
    return h.reshape(batch, seq, D_MODEL)
```

```python
import functools

import jax
import jax.numpy as jnp
from jax import lax
from jax.experimental import pallas as pl
from jax.experimental.pallas import tpu as pltpu

F32 = jnp.float32
BF16 = jnp.bfloat16

D_MODEL = 1024
D_FF = 2816
N_HEADS = 8
HEAD_DIM = 64
ATTN_WIDTH = N_HEADS * HEAD_DIM
N_GROUPS = 8
GROUP_DIM = 64
SGU_WIDTH = N_GROUPS * GROUP_DIM
CHUNK = 128
RMS_EPS = 1e-6
LN_EPS = 1e-5
LOG2E = 1.4426950408889634

LANES = 128
V7X_VMEM_BYTES = 64 * 1024 * 1024
VMEM_LIMIT = V7X_VMEM_BYTES * 7 // 8

TOKEN_TILE = 512
FFN_COLS = 256
ATTN_BLOCK = 256
Q_TILE = 512
V_ROWS = HEAD_DIM + 16
N_SPLIT = 3
QK_DEPTH = LANES
SGU_PACK = 4


def _rms_norm(x, gain):
    return (x * lax.rsqrt(jnp.mean(x * x, axis=-1, keepdims=True) + RMS_EPS)) * gain


def _gelu(x):
    return x * (lax.erf(x * (2.0 ** -0.5)) + 1.0) * 0.5


def _resident(shape):
    zeros = (0,) * len(shape)
    return pl.BlockSpec(shape, lambda *_: zeros, pipeline_mode=pl.Buffered(1))


def _params():
    return pltpu.CompilerParams(dimension_semantics=("arbitrary",),
                                vmem_limit_bytes=VMEM_LIMIT)


def _ffn_kernel(x_ref, gain_ref, wi_ref, wo_ref, fgain_ref, o_ref, act_ref, *, final_norm):
    x = x_ref[...]
    nb = _rms_norm(x, gain_ref[...]).astype(BF16)
    for j in range(D_FF // FFN_COLS):
        lo = j * FFN_COLS
        gate = jnp.dot(nb, wi_ref[:, lo:lo + FFN_COLS], preferred_element_type=F32)
        up = jnp.dot(nb, wi_ref[:, D_FF + lo:D_FF + lo + FFN_COLS], preferred_element_type=F32)
        act_ref[:, lo:lo + FFN_COLS] = (gate * jax.nn.sigmoid(gate) * up).astype(BF16)
    y = jnp.dot(act_ref[...], wo_ref[...], preferred_element_type=F32)
    h = x + 0.5 * y
    if final_norm:
        h = _rms_norm(h, fgain_ref[...])
    o_ref[...] = h


def _ffn(x, gain, wi, wo, fgain, *, final_norm):
    t = x.shape[0]
    tile = pl.BlockSpec((TOKEN_TILE, D_MODEL), lambda i: (i, 0))
    return pl.pallas_call(
        functools.partial(_ffn_kernel, final_norm=final_norm),
        grid=(t // TOKEN_TILE,),
        in_specs=[tile, _resident((1, D_MODEL)), _resident((D_MODEL, 2 * D_FF)),
                  _resident((D_FF, D_MODEL)), _resident((1, D_MODEL))],
        out_specs=tile,
        out_shape=jax.ShapeDtypeStruct((t, D_MODEL), F32),
        scratch_shapes=[pltpu.VMEM((TOKEN_TILE, D_FF), BF16)],
        compiler_params=_params(),
        name="ffn_final" if final_norm else "ffn",
    )(x, gain, wi, wo, fgain)


def _split_bf16(x):
    parts = []
    for _ in range(N_SPLIT):
        p = x.astype(BF16).astype(F32)
        parts.append(p)
        x = x - p
    return parts


def _proj_kernel(h_ref, gain_ref, wqt_ref, wvt_ref, wnt_ref, bf_ref, lng_ref, lnb_ref,
                 wsp_ref, bsp_ref,
                 q_ref, kx_ref, v_ref, sgu_ref, carry_ref, *, tiles_per_seq):
    i = pl.program_id(0)

    @pl.when(i % tiles_per_seq == 0)
    def _():
        carry_ref[...] = jnp.zeros_like(carry_ref)

    nb = _rms_norm(h_ref[...], gain_ref[...]).astype(BF16)

    tdims = (((1,), (1,)), ((), ()))
    qt = lax.dot_general(wqt_ref[...], nb, tdims, preferred_element_type=F32)
    qt = (qt * LOG2E).astype(BF16)
    sel_row = lax.broadcasted_iota(jnp.int32, (QK_DEPTH - HEAD_DIM, TOKEN_TILE), 0)
    ones_rows = jnp.where(sel_row < N_SPLIT, 1.0, 0.0).astype(BF16)
    for hd in range(N_HEADS):
        q_ref[0, hd, :HEAD_DIM, :] = qt[hd * HEAD_DIM:(hd + 1) * HEAD_DIM, :]
        q_ref[0, hd, HEAD_DIM:, :] = ones_rows
    vt = lax.dot_general(wvt_ref[...], nb, tdims, preferred_element_type=F32).astype(BF16)
    pad_row = lax.broadcasted_iota(jnp.int32, (V_ROWS - HEAD_DIM, ATTN_BLOCK), 0)
    ones_row = jnp.where(pad_row == 0, 1.0, 0.0).astype(BF16)
    for s in range(TOKEN_TILE // ATTN_BLOCK):
        for hd in range(N_HEADS):
            v_ref[0, hd, s, :HEAD_DIM, :] = vt[hd * HEAD_DIM:(hd + 1) * HEAD_DIM,
                                               s * ATTN_BLOCK:(s + 1) * ATTN_BLOCK]
            v_ref[0, hd, s, HEAD_DIM:, :] = ones_row

    z = lax.dot_general(nb, wnt_ref[...], tdims, preferred_element_type=F32)
    zk = z[:, :ATTN_WIDTH]
    u = z[:, ATTN_WIDTH:ATTN_WIDTH + SGU_WIDTH]
    vg = z[:, ATTN_WIDTH + SGU_WIDTH:ATTN_WIDTH + 2 * SGU_WIDTH]
    f = z[:, ATTN_WIDTH + 2 * SGU_WIDTH:] + bf_ref[...]
    logf = jnp.minimum(f, 0.0) - jnp.log1p(jnp.exp(-jnp.abs(f)))

    lane = lax.broadcasted_iota(jnp.int32, (CHUNK, LANES), 1)
    term = lane % N_SPLIT
    live = lane < N_SPLIT * N_HEADS

    def by_term(parts):
        out = parts[-1]
        for j in range(N_SPLIT - 2, -1, -1):
            out = jnp.where(term == j, parts[j], out)
        return out

    row = lax.broadcasted_iota(jnp.int32, (CHUNK, CHUNK), 0)
    col = lax.broadcasted_iota(jnp.int32, (CHUNK, CHUNK), 1)
    tri = jnp.where(row >= col, 1.0, 0.0).astype(BF16)
    carry = carry_ref[...]
    for c in range(TOKEN_TILE // CHUNK):
        rows = slice(c * CHUNK, (c + 1) * CHUNK)
        local = jnp.dot(tri, by_term(_split_bf16(logf[rows])).astype(BF16),
                        preferred_element_type=F32)
        shifted = {0: local}
        for d in range(1, N_SPLIT):
            shifted[d] = pltpu.roll(local, d, 1)
            shifted[-d] = pltpu.roll(local, LANES - d, 1)
        csum = carry
        for j in range(N_SPLIT):
            csum = csum + by_term([shifted[t - j] for t in range(N_SPLIT)])
        carry = csum[CHUNK - 1:, :]
        bias = jnp.where(live, by_term(_split_bf16(csum * -LOG2E)), 0.0)
        for hd in range(N_HEADS):
            pair = hd // 2
            kslab = zk[rows, pair * LANES:(pair + 1) * LANES]
            if hd % 2:
                kslab = pltpu.roll(kslab, HEAD_DIM, 1)
            hbias = pltpu.roll(bias, HEAD_DIM - N_SPLIT * hd, 1)
            kx = jnp.where(lane < HEAD_DIM, kslab,
                           jnp.where(lane < HEAD_DIM + N_SPLIT, hbias, 0.0))
            kx_ref[0, hd, rows, :] = kx.astype(BF16)
    carry_ref[...] = carry

    gu = _gelu(u)
    gv = _gelu(vg)
    mu = jnp.mean(gv, axis=-1, keepdims=True)
    dv = gv - mu
    var = jnp.mean(dv * dv, axis=-1, keepdims=True)
    vn = ((dv * lax.rsqrt(var + LN_EPS)) * lng_ref[...] + lnb_ref[...]).astype(BF16)
    pack_w = SGU_PACK * GROUP_DIM
    wrow = lax.broadcasted_iota(jnp.int32, (CHUNK, SGU_PACK * CHUNK), 0)
    wcol = lax.broadcasted_iota(jnp.int32, (CHUNK, SGU_PACK * CHUNK), 1) % CHUNK
    wsp = [jnp.where(wcol <= wrow, wsp_ref[p], 0.0).astype(BF16)
           for p in range(N_GROUPS // SGU_PACK)]
    lane_group = lax.broadcasted_iota(jnp.int32, (CHUNK, pack_w), 1) // GROUP_DIM
    for c in range(TOKEN_TILE // CHUNK):
        rows = slice(c * CHUNK, (c + 1) * CHUNK)
        mixed = []
        for p in range(N_GROUPS // SGU_PACK):
            slab = vn[rows, p * pack_w:(p + 1) * pack_w]
            blocks = jnp.concatenate(
                [jnp.where(lane_group == r, slab, jnp.zeros_like(slab)) for r in range(SGU_PACK)],
                axis=0)
            mixed.append(jnp.dot(wsp[p], blocks, preferred_element_type=F32))
        mixed = jnp.concatenate(mixed, axis=1) + bsp_ref[...]
        sgu_ref[rows, :] = (gu[rows] * mixed).astype(BF16)


def _proj(h, gain, wqt, wvt, wnt, bf, lng, lnb, wsp, bsp, *, batch, seq):
    t = batch * seq
    tiles_per_seq = seq // TOKEN_TILE
    sub = TOKEN_TILE // ATTN_BLOCK
    tok = lambda i: (i, 0)
    return pl.pallas_call(
        functools.partial(_proj_kernel, tiles_per_seq=tiles_per_seq),
        grid=(t // TOKEN_TILE,),
        in_specs=[pl.BlockSpec((TOKEN_TILE, D_MODEL), tok), _resident((1, D_MODEL)),
                  _resident(wqt.shape), _resident(wvt.shape), _resident(wnt.shape),
                  _resident((1, LANES)), _resident((1, SGU_WIDTH)), _resident((1, SGU_WIDTH)),
                  _resident(wsp.shape), _resident(bsp.shape)],
        out_specs=[pl.BlockSpec((1, N_HEADS, QK_DEPTH, TOKEN_TILE),
                                lambda i: (i // tiles_per_seq, 0, 0, i % tiles_per_seq)),
                   pl.BlockSpec((1, N_HEADS, TOKEN_TILE, LANES),
                                lambda i: (i // tiles_per_seq, 0, i % tiles_per_seq, 0)),
                   pl.BlockSpec((1, N_HEADS, sub, V_ROWS, ATTN_BLOCK),
                                lambda i: (i // tiles_per_seq, 0, i % tiles_per_seq, 0, 0)),
                   pl.BlockSpec((TOKEN_TILE, SGU_WIDTH), tok)],
        out_shape=[jax.ShapeDtypeStruct((batch, N_HEADS, QK_DEPTH, seq), BF16),
                   jax.ShapeDtypeStruct((batch, N_HEADS, seq, LANES), BF16),
                   jax.ShapeDtypeStruct((batch, N_HEADS, seq // ATTN_BLOCK, V_ROWS, ATTN_BLOCK), BF16),
                   jax.ShapeDtypeStruct((t, SGU_WIDTH), BF16)],
        scratch_shapes=[pltpu.VMEM((1, LANES), F32)],
        compiler_params=_params(),
        name="proj",
    )(h, gain, wqt, wvt, wnt, bf, lng, lnb, wsp, bsp)


def _attn_kernel(q_ref, kx_ref, v_ref, o_ref, s_ref, smax_ref, m_ref, acc_ref):
    blk = ATTN_BLOCK
    n_diag = Q_TILE // blk
    n_off = pl.program_id(1) * n_diag

    m_ref[...] = jnp.full(m_ref.shape, -jnp.inf, F32)
    acc_ref[...] = jnp.zeros(acc_ref.shape, F32)

    key_pos = lax.broadcasted_iota(jnp.int32, (blk, Q_TILE), 0)
    qry_pos = lax.broadcasted_iota(jnp.int32, (blk, Q_TILE), 1)
    causal = key_pos <= qry_pos

    def scores(j, hd, diag):
        rows = pl.ds(pl.multiple_of(j * blk, blk), blk)
        lo = 0 if diag is None else diag * blk
        st = jnp.dot(kx_ref[0, hd, rows, :], q_ref[0, hd, :, lo:], preferred_element_type=F32)
        if diag is not None:
            st = jnp.where(causal[:, :Q_TILE - lo], st, -jnp.inf)
        if lo:
            s_ref[hd, :, :lo] = jnp.full((blk, lo), -jnp.inf, F32)
            smax_ref[hd, :, :lo] = jnp.full((1, lo), -jnp.inf, F32)
        s_ref[hd, :, lo:] = st
        smax_ref[hd, :, lo:] = jnp.max(st, axis=0, keepdims=True)

    def step(j, j_next, next_diag=None):
        for hd in range(N_HEADS):
            st = s_ref[hd]
            m_old = m_ref[hd]
            m_new = jnp.maximum(m_old, smax_ref[hd])
            if j_next is not None:
                scores(j_next, hd, next_diag)
            alpha = jnp.exp2(m_old - m_new)
            pt = jnp.exp2(st - m_new).astype(BF16)
            pv = jnp.dot(v_ref[0, hd, j], pt, preferred_element_type=F32)
            acc_ref[hd] = acc_ref[hd] * alpha + pv
            m_ref[hd] = m_new

    for hd in range(N_HEADS):
        scores(n_off, hd, 0)
    for d in range(1, n_diag):
        step(n_off + d - 1, n_off + d, d)

    def body(i, carry):
        j0 = i * n_diag
        step(jnp.where(i == 0, n_off + n_diag - 1, j0 - 1), j0)
        for d in range(1, n_diag):
            step(j0 + d - 1, j0 + d)
        return carry

    lax.fori_loop(0, pl.program_id(1), body, 0)
    step(jnp.where(n_off == 0, n_diag - 1, n_off - 1), None)

    for hd in range(N_HEADS):
        acc = acc_ref[hd]
        o_ref[0, hd * HEAD_DIM:(hd + 1) * HEAD_DIM, :] = (
            acc[:HEAD_DIM] / acc[HEAD_DIM:HEAD_DIM + 1]).astype(BF16)


def _attn(q, kx, v, *, batch, seq):
    nblk = seq // ATTN_BLOCK
    return pl.pallas_call(
        _attn_kernel,
        grid=(batch, seq // Q_TILE),
        in_specs=[pl.BlockSpec((1, N_HEADS, QK_DEPTH, Q_TILE), lambda b, i: (b, 0, 0, i)),
                  pl.BlockSpec((1, N_HEADS, seq, LANES), lambda b, i: (b, 0, 0, 0)),
                  pl.BlockSpec((1, N_HEADS, nblk, V_ROWS, ATTN_BLOCK), lambda b, i: (b, 0, 0, 0, 0))],
        out_specs=pl.BlockSpec((1, ATTN_WIDTH, Q_TILE), lambda b, i: (b, 0, i)),
        out_shape=jax.ShapeDtypeStruct((batch, ATTN_WIDTH, seq), BF16),
        scratch_shapes=[pltpu.VMEM((N_HEADS, ATTN_BLOCK, Q_TILE), F32),
                        pltpu.VMEM((N_HEADS, 1, Q_TILE), F32),
                        pltpu.VMEM((N_HEADS, 1, Q_TILE), F32),
                        pltpu.VMEM((N_HEADS, V_ROWS, Q_TILE), F32)],
        compiler_params=pltpu.CompilerParams(dimension_semantics=("arbitrary", "arbitrary"),
                                             vmem_limit_bytes=VMEM_LIMIT),
        name="attn",
    )(q, kx, v)


def _merge_kernel(h_ref, at_ref, sgu_ref, gain_ref, wgatet_ref, bgate_ref, wua_ref, wug_ref,
                  wout_ref, o_ref):
    h = h_ref[...]
    nb = _rms_norm(h, gain_ref[...]).astype(BF16)
    logits = lax.dot_general(nb, wgatet_ref[...], (((1,), (1,)), ((), ())),
                             preferred_element_type=F32)
    gates = jax.nn.sigmoid(logits + bgate_ref[...])
    up_a = lax.dot_general(at_ref[0], wua_ref[...], (((0,), (0,)), ((), ())),
                           preferred_element_type=F32)
    up_g = jnp.dot(sgu_ref[...], wug_ref[...], preferred_element_type=F32)
    merged = gates[:, :D_MODEL] * up_a + gates[:, D_MODEL:] * up_g
    o_ref[...] = h + jnp.dot(merged.astype(BF16), wout_ref[...], preferred_element_type=F32)


def _merge(h, at, sgu, gain, wgate, bgate, wua, wug, wout, *, seq):
    t = h.shape[0]
    tiles_per_seq = seq // TOKEN_TILE
    tok = lambda i: (i, 0)
    return pl.pallas_call(
        _merge_kernel,
        grid=(t // TOKEN_TILE,),
        in_specs=[pl.BlockSpec((TOKEN_TILE, D_MODEL), tok),
                  pl.BlockSpec((1, ATTN_WIDTH, TOKEN_TILE),
                               lambda i: (i // tiles_per_seq, 0, i % tiles_per_seq)),
                  pl.BlockSpec((TOKEN_TILE, SGU_WIDTH), tok),
                  _resident((1, D_MODEL)), _resident(wgate.shape), _resident(bgate.shape),
                  _resident(wua.shape), _resident(wug.shape), _resident(wout.shape)],
        out_specs=pl.BlockSpec((TOKEN_TILE, D_MODEL), tok),
        out_shape=jax.ShapeDtypeStruct((t, D_MODEL), F32),
        compiler_params=_params(),
        name="merge",
    )(h, at, sgu, gain, wgate, bgate, wua, wug, wout)


def kernel(x, ffn1_norm, ffn1_w_in, ffn1_w_out, mix_norm, w_in, b_forget, b_gate, sgu_ln_gain,
           sgu_ln_bias, sgu_w_spatial, sgu_b_spatial, w_up_attn, w_up_sgu, w_out, ffn2_norm,
           ffn2_w_in, ffn2_w_out, final_norm):
    batch, seq, _ = x.shape
    depth = ffn1_norm.shape[0]
    row = lambda v: v.reshape(1, -1)
    fgain = row(final_norm)
    h = x.reshape(batch * seq, D_MODEL)
    for l in range(depth):
        last = l == depth - 1
        h = _ffn(h, row(ffn1_norm[l]), ffn1_w_in[l].astype(BF16), ffn1_w_out[l].astype(BF16),
                 fgain, final_norm=False)

        o_k, o_v, o_f = ATTN_WIDTH, 2 * ATTN_WIDTH, 3 * ATTN_WIDTH
        o_u = o_f + N_HEADS
        o_vg, o_g = o_u + SGU_WIDTH, o_u + 2 * SGU_WIDTH
        wt = w_in[l].T
        wqt = (wt[:o_k] * (HEAD_DIM ** -0.5)).astype(BF16)
        wvt = wt[o_v:o_f].astype(BF16)
        wft = jnp.repeat(wt[o_f:o_u], N_SPLIT, axis=0)
        wft = jnp.pad(wft, ((0, LANES - N_SPLIT * N_HEADS), (0, 0)))
        wnt = jnp.concatenate([wt[o_k:o_v], wt[o_u:o_g], wft], axis=0).astype(BF16)
        bf = jnp.pad(jnp.repeat(b_forget[l], N_SPLIT), (0, LANES - N_SPLIT * N_HEADS)).reshape(1, LANES)
        wsp = sgu_w_spatial[l].reshape(N_GROUPS // SGU_PACK, SGU_PACK, CHUNK, CHUNK)
        wsp = wsp.transpose(0, 2, 1, 3).reshape(N_GROUPS // SGU_PACK, CHUNK, SGU_PACK * CHUNK)
        bsp = jnp.repeat(sgu_b_spatial[l].T, GROUP_DIM, axis=1)

        q, kx, v, sgu = _proj(h, row(mix_norm[l]), wqt, wvt, wnt, bf, row(sgu_ln_gain[l]),
                              row(sgu_ln_bias[l]), wsp, bsp, batch=batch, seq=seq)
        at = _attn(q, kx, v, batch=batch, seq=seq)
        h = _merge(h, at, sgu, row(mix_norm[l]), wt[o_g:].astype(BF16), row(b_gate[l]),
                   w_up_attn[l].astype(BF16), w_up_sgu[l].astype(BF16), w_out[l].astype(BF16),
                   seq=seq)
        h = _ffn(h, row(ffn2_norm[l]), ffn2_w_in[l].astype(BF16), ffn2_w_out[l].astype(BF16),
                 fgain, final_norm=last)
    return h.reshape(batch, seq, D_MODEL)
```

```python
import functools

import jax
import jax.numpy as jnp
from jax import lax
from jax.experimental import pallas as pl
from jax.experimental.pallas import tpu as pltpu

F32 = jnp.float32
BF16 = jnp.bfloat16

D_MODEL = 1024
D_FF = 2816
N_HEADS = 8
HEAD_DIM = 64
ATTN_WIDTH = N_HEADS * HEAD_DIM
N_GROUPS = 8
GROUP_DIM = 64
SGU_WIDTH = N_GROUPS * GROUP_DIM
CHUNK = 128
RMS_EPS = 1e-6
LN_EPS = 1e-5
LOG2E = 1.4426950408889634

LANES = 128
V7X_VMEM_BYTES = 64 * 1024 * 1024
VMEM_LIMIT = V7X_VMEM_BYTES * 7 // 8

TOKEN_TILE = 512
FFN_COLS = 256
ATTN_BLOCK = 256
Q_TILE = 512
V_ROWS = HEAD_DIM + 16
N_SPLIT = 3
QK_DEPTH = LANES
SGU_PACK = 4


def _rms_norm(x, gain):
    return (x * lax.rsqrt(jnp.mean(x * x, axis=-1, keepdims=True) + RMS_EPS)) * gain


def _gelu(x):
    return x * (lax.erf(x * (2.0 ** -0.5)) + 1.0) * 0.5


def _resident(shape):
    zeros = (0,) * len(shape)
    return pl.BlockSpec(shape, lambda *_: zeros, pipeline_mode=pl.Buffered(1))


def _params():
    return pltpu.CompilerParams(dimension_semantics=("arbitrary",),
                                vmem_limit_bytes=VMEM_LIMIT)


def _ffn_kernel(x_ref, gain_ref, wi_ref, wo_ref, fgain_ref, o_ref, act_ref, *, final_norm):
    x = x_ref[...]
    n = _rms_norm(x, gain_ref[...])
    for j in range(D_FF // FFN_COLS):
        lo = j * FFN_COLS
        gate = jnp.dot(n, wi_ref[:, lo:lo + FFN_COLS], preferred_element_type=F32)
        up = jnp.dot(n, wi_ref[:, D_FF + lo:D_FF + lo + FFN_COLS], preferred_element_type=F32)
        act_ref[:, lo:lo + FFN_COLS] = gate * jax.nn.sigmoid(gate) * up
    y = jnp.dot(act_ref[...], wo_ref[...], preferred_element_type=F32)
    h = x + 0.5 * y
    if final_norm:
        h = _rms_norm(h, fgain_ref[...])
    o_ref[...] = h


def _ffn(x, gain, wi, wo, fgain, *, final_norm):
    t = x.shape[0]
    tile = pl.BlockSpec((TOKEN_TILE, D_MODEL), lambda i: (i, 0))
    return pl.pallas_call(
        functools.partial(_ffn_kernel, final_norm=final_norm),
        grid=(t // TOKEN_TILE,),
        in_specs=[tile, _resident((1, D_MODEL)), _resident((D_MODEL, 2 * D_FF)),
                  _resident((D_FF, D_MODEL)), _resident((1, D_MODEL))],
        out_specs=tile,
        out_shape=jax.ShapeDtypeStruct((t, D_MODEL), F32),
        scratch_shapes=[pltpu.VMEM((TOKEN_TILE, D_FF), F32)],
        compiler_params=_params(),
        name="ffn_final" if final_norm else "ffn",
    )(x, gain, wi, wo, fgain)


def _split_bf16(x):
    parts = []
    for _ in range(N_SPLIT):
        p = x.astype(BF16).astype(F32)
        parts.append(p)
        x = x - p
    return parts


def _proj_kernel(h_ref, gain_ref, wqt_ref, wvt_ref, wnt_ref, bf_ref, lng_ref, lnb_ref,
                 wsp_ref, bsp_ref,
                 q_ref, kx_ref, v_ref, sgu_ref, carry_ref, *, tiles_per_seq):
    i = pl.program_id(0)

    @pl.when(i % tiles_per_seq == 0)
    def _():
        carry_ref[...] = jnp.zeros_like(carry_ref)

    nb = _rms_norm(h_ref[...], gain_ref[...]).astype(BF16)

    tdims = (((1,), (1,)), ((), ()))
    qt = lax.dot_general(wqt_ref[...], nb, tdims, preferred_element_type=F32)
    qt = (qt * LOG2E).astype(BF16)
    sel_row = lax.broadcasted_iota(jnp.int32, (QK_DEPTH - HEAD_DIM, TOKEN_TILE), 0)
    ones_rows = jnp.where(sel_row < N_SPLIT, 1.0, 0.0).astype(BF16)
    for hd in range(N_HEADS):
        q_ref[0, hd, :HEAD_DIM, :] = qt[hd * HEAD_DIM:(hd + 1) * HEAD_DIM, :]
        q_ref[0, hd, HEAD_DIM:, :] = ones_rows
    vt = lax.dot_general(wvt_ref[...], nb, tdims, preferred_element_type=F32).astype(BF16)
    pad_row = lax.broadcasted_iota(jnp.int32, (V_ROWS - HEAD_DIM, ATTN_BLOCK), 0)
    ones_row = jnp.where(pad_row == 0, 1.0, 0.0).astype(BF16)
    for s in range(TOKEN_TILE // ATTN_BLOCK):
        for hd in range(N_HEADS):
            v_ref[0, hd, s, :HEAD_DIM, :] = vt[hd * HEAD_DIM:(hd + 1) * HEAD_DIM,
                                               s * ATTN_BLOCK:(s + 1) * ATTN_BLOCK]
            v_ref[0, hd, s, HEAD_DIM:, :] = ones_row

    z = lax.dot_general(nb, wnt_ref[...], tdims, preferred_element_type=F32)
    zk = z[:, :ATTN_WIDTH]
    u = z[:, ATTN_WIDTH:ATTN_WIDTH + SGU_WIDTH]
    vg = z[:, ATTN_WIDTH + SGU_WIDTH:ATTN_WIDTH + 2 * SGU_WIDTH]
    f = z[:, ATTN_WIDTH + 2 * SGU_WIDTH:] + bf_ref[...]
    logf = jnp.minimum(f, 0.0) - jnp.log1p(jnp.exp(-jnp.abs(f)))

    lane = lax.broadcasted_iota(jnp.int32, (CHUNK, LANES), 1)
    term = lane % N_SPLIT
    live = lane < N_SPLIT * N_HEADS

    def by_term(parts):
        out = parts[-1]
        for j in range(N_SPLIT - 2, -1, -1):
            out = jnp.where(term == j, parts[j], out)
        return out

    row = lax.broadcasted_iota(jnp.int32, (CHUNK, CHUNK), 0)
    col = lax.broadcasted_iota(jnp.int32, (CHUNK, CHUNK), 1)
    tri = jnp.where(row >= col, 1.0, 0.0).astype(BF16)
    carry = carry_ref[...]
    for c in range(TOKEN_TILE // CHUNK):
        rows = slice(c * CHUNK, (c + 1) * CHUNK)
        local = jnp.dot(tri, by_term(_split_bf16(logf[rows])).astype(BF16),
                        preferred_element_type=F32)
        shifted = {0: local}
        for d in range(1, N_SPLIT):
            shifted[d] = pltpu.roll(local, d, 1)
            shifted[-d] = pltpu.roll(local, LANES - d, 1)
        csum = carry
        for j in range(N_SPLIT):
            csum = csum + by_term([shifted[t - j] for t in range(N_SPLIT)])
        carry = csum[CHUNK - 1:, :]
        bias = jnp.where(live, by_term(_split_bf16(csum * -LOG2E)), 0.0)
        for hd in range(N_HEADS):
            pair = hd // 2
            kslab = zk[rows, pair * LANES:(pair + 1) * LANES]
            if hd % 2:
                kslab = pltpu.roll(kslab, HEAD_DIM, 1)
            hbias = pltpu.roll(bias, HEAD_DIM - N_SPLIT * hd, 1)
            kx = jnp.where(lane < HEAD_DIM, kslab,
                           jnp.where(lane < HEAD_DIM + N_SPLIT, hbias, 0.0))
            kx_ref[0, hd, rows, :] = kx.astype(BF16)
    carry_ref[...] = carry

    gu = _gelu(u)
    gv = _gelu(vg)
    mu = jnp.mean(gv, axis=-1, keepdims=True)
    dv = gv - mu
    var = jnp.mean(dv * dv, axis=-1, keepdims=True)
    vn = ((dv * lax.rsqrt(var + LN_EPS)) * lng_ref[...] + lnb_ref[...]).astype(BF16)
    pack_w = SGU_PACK * GROUP_DIM
    wrow = lax.broadcasted_iota(jnp.int32, (CHUNK, SGU_PACK * CHUNK), 0)
    wcol = lax.broadcasted_iota(jnp.int32, (CHUNK, SGU_PACK * CHUNK), 1) % CHUNK
    wsp = [jnp.where(wcol <= wrow, wsp_ref[p], 0.0).astype(BF16)
           for p in range(N_GROUPS // SGU_PACK)]
    lane_group = lax.broadcasted_iota(jnp.int32, (CHUNK, pack_w), 1) // GROUP_DIM
    for c in range(TOKEN_TILE // CHUNK):
        rows = slice(c * CHUNK, (c + 1) * CHUNK)
        mixed = []
        for p in range(N_GROUPS // SGU_PACK):
            slab = vn[rows, p * pack_w:(p + 1) * pack_w]
            blocks = jnp.concatenate(
                [jnp.where(lane_group == r, slab, jnp.zeros_like(slab)) for r in range(SGU_PACK)],
                axis=0)
            mixed.append(jnp.dot(wsp[p], blocks, preferred_element_type=F32))
        mixed = jnp.concatenate(mixed, axis=1) + bsp_ref[...]
        sgu_ref[rows, :] = (gu[rows] * mixed).astype(BF16)


def _proj(h, gain, wqt, wvt, wnt, bf, lng, lnb, wsp, bsp, *, batch, seq):
    t = batch * seq
    tiles_per_seq = seq // TOKEN_TILE
    sub = TOKEN_TILE // ATTN_BLOCK
    tok = lambda i: (i, 0)
    return pl.pallas_call(
        functools.partial(_proj_kernel, tiles_per_seq=tiles_per_seq),
        grid=(t // TOKEN_TILE,),
        in_specs=[pl.BlockSpec((TOKEN_TILE, D_MODEL), tok), _resident((1, D_MODEL)),
                  _resident(wqt.shape), _resident(wvt.shape), _resident(wnt.shape),
                  _resident((1, LANES)), _resident((1, SGU_WIDTH)), _resident((1, SGU_WIDTH)),
                  _resident(wsp.shape), _resident(bsp.shape)],
        out_specs=[pl.BlockSpec((1, N_HEADS, QK_DEPTH, TOKEN_TILE),
                                lambda i: (i // tiles_per_seq, 0, 0, i % tiles_per_seq)),
                   pl.BlockSpec((1, N_HEADS, TOKEN_TILE, LANES),
                                lambda i: (i // tiles_per_seq, 0, i % tiles_per_seq, 0)),
                   pl.BlockSpec((1, N_HEADS, sub, V_ROWS, ATTN_BLOCK),
                                lambda i: (i // tiles_per_seq, 0, i % tiles_per_seq, 0, 0)),
                   pl.BlockSpec((TOKEN_TILE, SGU_WIDTH), tok)],
        out_shape=[jax.ShapeDtypeStruct((batch, N_HEADS, QK_DEPTH, seq), BF16),
                   jax.ShapeDtypeStruct((batch, N_HEADS, seq, LANES), BF16),
                   jax.ShapeDtypeStruct((batch, N_HEADS, seq // ATTN_BLOCK, V_ROWS, ATTN_BLOCK), BF16),
                   jax.ShapeDtypeStruct((t, SGU_WIDTH), BF16)],
        scratch_shapes=[pltpu.VMEM((1, LANES), F32)],
        compiler_params=_params(),
        name="proj",
    )(h, gain, wqt, wvt, wnt, bf, lng, lnb, wsp, bsp)


def _attn_kernel(q_ref, kx_ref, v_ref, o_ref, s_ref, smax_ref, m_ref, acc_ref):
    blk = ATTN_BLOCK
    n_diag = Q_TILE // blk
    n_off = pl.program_id(1) * n_diag

    m_ref[...] = jnp.full(m_ref.shape, -jnp.inf, F32)
    acc_ref[...] = jnp.zeros(acc_ref.shape, F32)

    key_pos = lax.broadcasted_iota(jnp.int32, (blk, Q_TILE), 0)
    qry_pos = lax.broadcasted_iota(jnp.int32, (blk, Q_TILE), 1)
    causal = key_pos <= qry_pos

    def scores(j, hd, diag):
        rows = pl.ds(pl.multiple_of(j * blk, blk), blk)
        lo = 0 if diag is None else diag * blk
        st = jnp.dot(kx_ref[0, hd, rows, :], q_ref[0, hd, :, lo:], preferred_element_type=F32)
        if diag is not None:
            st = jnp.where(causal[:, :Q_TILE - lo], st, -jnp.inf)
        if lo:
            s_ref[hd, :, :lo] = jnp.full((blk, lo), -jnp.inf, F32)
            smax_ref[hd, :, :lo] = jnp.full((1, lo), -jnp.inf, F32)
        s_ref[hd, :, lo:] = st
        smax_ref[hd, :, lo:] = jnp.max(st, axis=0, keepdims=True)

    def step(j, j_next, next_diag=None):
        for hd in range(N_HEADS):
            st = s_ref[hd]
            m_old = m_ref[hd]
            m_new = jnp.maximum(m_old, smax_ref[hd])
            if j_next is not None:
                scores(j_next, hd, next_diag)
            alpha = jnp.exp2(m_old - m_new)
            pt = jnp.exp2(st - m_new).astype(BF16)
            pv = jnp.dot(v_ref[0, hd, j], pt, preferred_element_type=F32)
            acc_ref[hd] = acc_ref[hd] * alpha + pv
            m_ref[hd] = m_new

    for hd in range(N_HEADS):
        scores(n_off, hd, 0)
    for d in range(1, n_diag):
        step(n_off + d - 1, n_off + d, d)

    def body(i, carry):
        j0 = i * n_diag
        step(jnp.where(i == 0, n_off + n_diag - 1, j0 - 1), j0)
        for d in range(1, n_diag):
            step(j0 + d - 1, j0 + d)
        return carry

    lax.fori_loop(0, pl.program_id(1), body, 0)
    step(jnp.where(n_off == 0, n_diag - 1, n_off - 1), None)

    for hd in range(N_HEADS):
        acc = acc_ref[hd]
        o_ref[0, hd * HEAD_DIM:(hd + 1) * HEAD_DIM, :] = (
            acc[:HEAD_DIM] / acc[HEAD_DIM:HEAD_DIM + 1]).astype(BF16)


def _attn(q, kx, v, *, batch, seq):
    nblk = seq // ATTN_BLOCK
    return pl.pallas_call(
        _attn_kernel,
        grid=(batch, seq // Q_TILE),
        in_specs=[pl.BlockSpec((1, N_HEADS, QK_DEPTH, Q_TILE), lambda b, i: (b, 0, 0, i)),
                  pl.BlockSpec((1, N_HEADS, seq, LANES), lambda b, i: (b, 0, 0, 0)),
                  pl.BlockSpec((1, N_HEADS, nblk, V_ROWS, ATTN_BLOCK), lambda b, i: (b, 0, 0, 0, 0))],
        out_specs=pl.BlockSpec((1, ATTN_WIDTH, Q_TILE), lambda b, i: (b, 0, i)),
        out_shape=jax.ShapeDtypeStruct((batch, ATTN_WIDTH, seq), BF16),
        scratch_shapes=[pltpu.VMEM((N_HEADS, ATTN_BLOCK, Q_TILE), F32),
                        pltpu.VMEM((N_HEADS, 1, Q_TILE), F32),
                        pltpu.VMEM((N_HEADS, 1, Q_TILE), F32),
                        pltpu.VMEM((N_HEADS, V_ROWS, Q_TILE), F32)],
        compiler_params=pltpu.CompilerParams(dimension_semantics=("arbitrary", "arbitrary"),
                                             vmem_limit_bytes=VMEM_LIMIT),
        name="attn",
    )(q, kx, v)


def _merge_kernel(h_ref, at_ref, sgu_ref, gain_ref, wgatet_ref, bgate_ref, wua_ref, wug_ref,
                  wout_ref, o_ref):
    h = h_ref[...]
    nb = _rms_norm(h, gain_ref[...]).astype(BF16)
    logits = lax.dot_general(nb, wgatet_ref[...], (((1,), (1,)), ((), ())),
                             preferred_element_type=F32)
    gates = jax.nn.sigmoid(logits + bgate_ref[...])
    up_a = lax.dot_general(at_ref[0], wua_ref[...], (((0,), (0,)), ((), ())),
                           preferred_element_type=F32)
    up_g = jnp.dot(sgu_ref[...], wug_ref[...], preferred_element_type=F32)
    merged = gates[:, :D_MODEL] * up_a + gates[:, D_MODEL:] * up_g
    o_ref[...] = h + jnp.dot(merged.astype(BF16), wout_ref[...], preferred_element_type=F32)


def _merge(h, at, sgu, gain, wgate, bgate, wua, wug, wout, *, seq):
    t = h.shape[0]
    tiles_per_seq = seq // TOKEN_TILE
    tok = lambda i: (i, 0)
    return pl.pallas_call(
        _merge_kernel,
        grid=(t // TOKEN_TILE,),
        in_specs=[pl.BlockSpec((TOKEN_TILE, D_MODEL), tok),
                  pl.BlockSpec((1, ATTN_WIDTH, TOKEN_TILE),
                               lambda i: (i // tiles_per_seq, 0, i % tiles_per_seq)),
                  pl.BlockSpec((TOKEN_TILE, SGU_WIDTH), tok),
                  _resident((1, D_MODEL)), _resident(wgate.shape), _resident(bgate.shape),
                  _resident(wua.shape), _resident(wug.shape), _resident(wout.shape)],
        out_specs=pl.BlockSpec((TOKEN_TILE, D_MODEL), tok),
        out_shape=jax.ShapeDtypeStruct((t, D_MODEL), F32),
        compiler_params=_params(),
        name="merge",
    )(h, at, sgu, gain, wgate, bgate, wua, wug, wout)


def kernel(x, ffn1_norm, ffn1_w_in, ffn1_w_out, mix_norm, w_in, b_forget, b_gate, sgu_ln_gain,
           sgu_ln_bias, sgu_w_spatial, sgu_b_spatial, w_up_attn, w_up_sgu, w_out, ffn2_norm,
           ffn2_w_in, ffn2_w_out, final_norm):
    batch, seq, _ = x.shape
    depth = ffn1_norm.shape[0]
    row = lambda v: v.reshape(1, -1)
    fgain = row(final_norm)
    h = x.reshape(batch * seq, D_MODEL)
    for l in range(depth):
        last = l == depth - 1
        h = _ffn(h, row(ffn1_norm[l]), ffn1_w_in[l], ffn1_w_out[l], fgain, final_norm=False)

        o_k, o_v, o_f = ATTN_WIDTH, 2 * ATTN_WIDTH, 3 * ATTN_WIDTH
        o_u = o_f + N_HEADS
        o_vg, o_g = o_u + SGU_WIDTH, o_u + 2 * SGU_WIDTH
        wt = w_in[l].T
        wqt = (wt[:o_k] * (HEAD_DIM ** -0.5)).astype(BF16)
        wvt = wt[o_v:o_f].astype(BF16)
        wft = jnp.repeat(wt[o_f:o_u], N_SPLIT, axis=0)
        wft = jnp.pad(wft, ((0, LANES - N_SPLIT * N_HEADS), (0, 0)))
        wnt = jnp.concatenate([wt[o_k:o_v], wt[o_u:o_g], wft], axis=0).astype(BF16)
        bf = jnp.pad(jnp.repeat(b_forget[l], N_SPLIT), (0, LANES - N_SPLIT * N_HEADS)).reshape(1, LANES)
        wsp = sgu_w_spatial[l].reshape(N_GROUPS // SGU_PACK, SGU_PACK, CHUNK, CHUNK)
        wsp = wsp.transpose(0, 2, 1, 3).reshape(N_GROUPS // SGU_PACK, CHUNK, SGU_PACK * CHUNK)
        bsp = jnp.repeat(sgu_b_spatial[l].T, GROUP_DIM, axis=1)

        q, kx, v, sgu = _proj(h, row(mix_norm[l]), wqt, wvt, wnt, bf, row(sgu_ln_gain[l]),
                              row(sgu_ln_bias[l]), wsp, bsp, batch=batch, seq=seq)
        at = _attn(q, kx, v, batch=batch, seq=seq)
        h = _merge(h, at, sgu, row(mix_norm[l]), wt[o_g:].astype(BF16), row(b_gate[l]),
                   w_up_attn[l].astype(BF16), w_up_sgu[l].astype(BF16), w_out[l].astype(BF16),
                   seq=seq)
        h = _ffn(h, row(ffn2_norm[l]), ffn2_w_in[l], ffn2_w_out[l], fgain, final_norm=last)
    return h.reshape(batch, seq, D_MODEL)
```

```python
import functools

import jax
import jax.numpy as jnp
from jax import lax
from jax.experimental import pallas as pl
from jax.experimental.pallas import tpu as pltpu

F32 = jnp.float32
BF16 = jnp.bfloat16

D_MODEL = 1024
D_FF = 2816
N_HEADS = 8
HEAD_DIM = 64
ATTN_WIDTH = N_HEADS * HEAD_DIM
N_GROUPS = 8
GROUP_DIM = 64
SGU_WIDTH = N_GROUPS * GROUP_DIM
CHUNK = 128
RMS_EPS = 1e-6
LN_EPS = 1e-5
LOG2E = 1.4426950408889634

LANES = 128
V7X_VMEM_BYTES = 64 * 1024 * 1024
VMEM_LIMIT = V7X_VMEM_BYTES * 7 // 8

TOKEN_TILE = 512
FFN_COLS = 256
ATTN_BLOCK = 256
Q_TILE = 512
LOOP_GROUPS = 2
V_ROWS = HEAD_DIM + 16
N_SPLIT = 3
QK_DEPTH = LANES
SGU_PACK = 4


def _rms_norm(x, gain):
    return (x * lax.rsqrt(jnp.mean(x * x, axis=-1, keepdims=True) + RMS_EPS)) * gain


def _gelu(x):
    return x * (lax.erf(x * (2.0 ** -0.5)) + 1.0) * 0.5


def _resident(shape):
    zeros = (0,) * len(shape)
    return pl.BlockSpec(shape, lambda *_: zeros, pipeline_mode=pl.Buffered(1))


def _params():
    return pltpu.CompilerParams(dimension_semantics=("arbitrary",),
                                vmem_limit_bytes=VMEM_LIMIT)


def _ffn_kernel(x_ref, gain_ref, wi_ref, wo_ref, fgain_ref, o_ref, act_ref, *, final_norm):
    x = x_ref[...]
    n = _rms_norm(x, gain_ref[...])
    for j in range(D_FF // FFN_COLS):
        lo = j * FFN_COLS
        gate = jnp.dot(n, wi_ref[:, lo:lo + FFN_COLS], preferred_element_type=F32)
        up = jnp.dot(n, wi_ref[:, D_FF + lo:D_FF + lo + FFN_COLS], preferred_element_type=F32)
        act_ref[:, lo:lo + FFN_COLS] = gate * jax.nn.sigmoid(gate) * up
    y = jnp.dot(act_ref[...], wo_ref[...], preferred_element_type=F32)
    h = x + 0.5 * y
    if final_norm:
        h = _rms_norm(h, fgain_ref[...])
    o_ref[...] = h


def _ffn(x, gain, wi, wo, fgain, *, final_norm):
    t = x.shape[0]
    tile = pl.BlockSpec((TOKEN_TILE, D_MODEL), lambda i: (i, 0))
    return pl.pallas_call(
        functools.partial(_ffn_kernel, final_norm=final_norm),
        grid=(t // TOKEN_TILE,),
        in_specs=[tile, _resident((1, D_MODEL)), _resident((D_MODEL, 2 * D_FF)),
                  _resident((D_FF, D_MODEL)), _resident((1, D_MODEL))],
        out_specs=tile,
        out_shape=jax.ShapeDtypeStruct((t, D_MODEL), F32),
        scratch_shapes=[pltpu.VMEM((TOKEN_TILE, D_FF), F32)],
        compiler_params=_params(),
        name="ffn_final" if final_norm else "ffn",
    )(x, gain, wi, wo, fgain)


def _split_bf16(x):
    parts = []
    for _ in range(N_SPLIT):
        p = x.astype(BF16).astype(F32)
        parts.append(p)
        x = x - p
    return parts


def _proj_kernel(h_ref, gain_ref, wqt_ref, wvt_ref, wnt_ref, bf_ref, lng_ref, lnb_ref,
                 wsp_ref, bsp_ref,
                 q_ref, kx_ref, v_ref, sgu_ref, carry_ref, *, tiles_per_seq):
    i = pl.program_id(0)

    @pl.when(i % tiles_per_seq == 0)
    def _():
        carry_ref[...] = jnp.zeros_like(carry_ref)

    tdims = (((1,), (1,)), ((), ()))
    sub = ATTN_BLOCK
    sel_row = lax.broadcasted_iota(jnp.int32, (QK_DEPTH - HEAD_DIM, sub), 0)
    ones_rows = jnp.where(sel_row < N_SPLIT, 1.0, 0.0).astype(BF16)
    pad_row = lax.broadcasted_iota(jnp.int32, (V_ROWS - HEAD_DIM, sub), 0)
    ones_row = jnp.where(pad_row == 0, 1.0, 0.0).astype(BF16)
    lane = lax.broadcasted_iota(jnp.int32, (CHUNK, LANES), 1)
    term = lane % N_SPLIT
    live = lane < N_SPLIT * N_HEADS

    def by_term(parts):
        out = parts[-1]
        for j in range(N_SPLIT - 2, -1, -1):
            out = jnp.where(term == j, parts[j], out)
        return out

    row = lax.broadcasted_iota(jnp.int32, (CHUNK, CHUNK), 0)
    col = lax.broadcasted_iota(jnp.int32, (CHUNK, CHUNK), 1)
    tri = jnp.where(row >= col, 1.0, 0.0).astype(BF16)
    pack_w = SGU_PACK * GROUP_DIM
    wrow = lax.broadcasted_iota(jnp.int32, (CHUNK, SGU_PACK * CHUNK), 0)
    wcol = lax.broadcasted_iota(jnp.int32, (CHUNK, SGU_PACK * CHUNK), 1) % CHUNK
    wsp = [jnp.where(wcol <= wrow, wsp_ref[p], 0.0).astype(BF16)
           for p in range(N_GROUPS // SGU_PACK)]
    lane_group = lax.broadcasted_iota(jnp.int32, (CHUNK, pack_w), 1) // GROUP_DIM

    zs = []
    for s in range(TOKEN_TILE // sub):
        r0 = s * sub
        nb = _rms_norm(h_ref[r0:r0 + sub, :], gain_ref[...]).astype(BF16)

        qt = lax.dot_general(wqt_ref[...], nb, tdims, preferred_element_type=F32)
        qt = (qt * LOG2E).astype(BF16)
        vt = lax.dot_general(wvt_ref[...], nb, tdims, preferred_element_type=F32).astype(BF16)
        for hd in range(N_HEADS):
            q_ref[0, hd, :HEAD_DIM, r0:r0 + sub] = qt[hd * HEAD_DIM:(hd + 1) * HEAD_DIM, :]
            q_ref[0, hd, HEAD_DIM:, r0:r0 + sub] = ones_rows
            v_ref[0, hd, s, :HEAD_DIM, :] = vt[hd * HEAD_DIM:(hd + 1) * HEAD_DIM, :]
            v_ref[0, hd, s, HEAD_DIM:, :] = ones_row
        zs.append(lax.dot_general(nb, wnt_ref[...], tdims, preferred_element_type=F32))

    carry = carry_ref[...]
    for s, z in enumerate(zs):
        r0 = s * sub
        zk = z[:, :ATTN_WIDTH]
        u = z[:, ATTN_WIDTH:ATTN_WIDTH + SGU_WIDTH]
        vg = z[:, ATTN_WIDTH + SGU_WIDTH:ATTN_WIDTH + 2 * SGU_WIDTH]
        f = z[:, ATTN_WIDTH + 2 * SGU_WIDTH:] + bf_ref[...]
        logf = jnp.minimum(f, 0.0) - jnp.log1p(jnp.exp(-jnp.abs(f)))

        gu = _gelu(u)
        gv = _gelu(vg)
        mu = jnp.mean(gv, axis=-1, keepdims=True)
        dv = gv - mu
        var = jnp.mean(dv * dv, axis=-1, keepdims=True)
        vn = ((dv * lax.rsqrt(var + LN_EPS)) * lng_ref[...] + lnb_ref[...]).astype(BF16)

        for c in range(sub // CHUNK):
            rows = slice(c * CHUNK, (c + 1) * CHUNK)
            out_rows = slice(r0 + c * CHUNK, r0 + (c + 1) * CHUNK)
            local = jnp.dot(tri, by_term(_split_bf16(logf[rows])).astype(BF16),
                            preferred_element_type=F32)
            shifted = {0: local}
            for d in range(1, N_SPLIT):
                shifted[d] = pltpu.roll(local, d, 1)
                shifted[-d] = pltpu.roll(local, LANES - d, 1)
            csum = carry
            for j in range(N_SPLIT):
                csum = csum + by_term([shifted[t - j] for t in range(N_SPLIT)])
            carry = csum[CHUNK - 1:, :]
            bias = jnp.where(live, by_term(_split_bf16(csum * -LOG2E)), 0.0)
            for hd in range(N_HEADS):
                pair = hd // 2
                kslab = zk[rows, pair * LANES:(pair + 1) * LANES]
                if hd % 2:
                    kslab = pltpu.roll(kslab, HEAD_DIM, 1)
                hbias = pltpu.roll(bias, HEAD_DIM - N_SPLIT * hd, 1)
                kx = jnp.where(lane < HEAD_DIM, kslab,
                               jnp.where(lane < HEAD_DIM + N_SPLIT, hbias, 0.0))
                kx_ref[0, hd, out_rows, :] = kx.astype(BF16)

            mixed = []
            for p in range(N_GROUPS // SGU_PACK):
                slab = vn[rows, p * pack_w:(p + 1) * pack_w]
                blocks = jnp.concatenate(
                    [jnp.where(lane_group == r, slab, jnp.zeros_like(slab))
                     for r in range(SGU_PACK)], axis=0)
                mixed.append(jnp.dot(wsp[p], blocks, preferred_element_type=F32))
            mixed = jnp.concatenate(mixed, axis=1) + bsp_ref[...]
            sgu_ref[out_rows, :] = (gu[rows] * mixed).astype(BF16)
    carry_ref[...] = carry


def _proj(h, gain, wqt, wvt, wnt, bf, lng, lnb, wsp, bsp, *, batch, seq):
    t = batch * seq
    tiles_per_seq = seq // TOKEN_TILE
    sub = TOKEN_TILE // ATTN_BLOCK
    tok = lambda i: (i, 0)
    return pl.pallas_call(
        functools.partial(_proj_kernel, tiles_per_seq=tiles_per_seq),
        grid=(t // TOKEN_TILE,),
        in_specs=[pl.BlockSpec((TOKEN_TILE, D_MODEL), tok), _resident((1, D_MODEL)),
                  _resident(wqt.shape), _resident(wvt.shape), _resident(wnt.shape),
                  _resident((1, LANES)), _resident((1, SGU_WIDTH)), _resident((1, SGU_WIDTH)),
                  _resident(wsp.shape), _resident(bsp.shape)],
        out_specs=[pl.BlockSpec((1, N_HEADS, QK_DEPTH, TOKEN_TILE),
                                lambda i: (i // tiles_per_seq, 0, 0, i % tiles_per_seq)),
                   pl.BlockSpec((1, N_HEADS, TOKEN_TILE, LANES),
                                lambda i: (i // tiles_per_seq, 0, i % tiles_per_seq, 0)),
                   pl.BlockSpec((1, N_HEADS, sub, V_ROWS, ATTN_BLOCK),
                                lambda i: (i // tiles_per_seq, 0, i % tiles_per_seq, 0, 0)),
                   pl.BlockSpec((TOKEN_TILE, SGU_WIDTH), tok)],
        out_shape=[jax.ShapeDtypeStruct((batch, N_HEADS, QK_DEPTH, seq), BF16),
                   jax.ShapeDtypeStruct((batch, N_HEADS, seq, LANES), BF16),
                   jax.ShapeDtypeStruct((batch, N_HEADS, seq // ATTN_BLOCK, V_ROWS, ATTN_BLOCK), BF16),
                   jax.ShapeDtypeStruct((t, SGU_WIDTH), BF16)],
        scratch_shapes=[pltpu.VMEM((1, LANES), F32)],
        compiler_params=_params(),
        name="proj",
    )(h, gain, wqt, wvt, wnt, bf, lng, lnb, wsp, bsp)


def _attn_kernel(q_ref, qnext_ref, kx_ref, v_ref, o_ref, s_ref, smax_ref, m_ref, acc_ref, *, nblk):
    blk = ATTN_BLOCK
    n_diag = Q_TILE // blk
    qi = pl.program_id(1)
    n_off = qi * n_diag

    m_ref[...] = jnp.full(m_ref.shape, -jnp.inf, F32)
    acc_ref[...] = jnp.zeros(acc_ref.shape, F32)

    key_pos = lax.broadcasted_iota(jnp.int32, (blk, Q_TILE), 0)
    qry_pos = lax.broadcasted_iota(jnp.int32, (blk, Q_TILE), 1)
    causal = key_pos <= qry_pos

    def scores(j, hd, diag, qt_ref=q_ref):
        rows = pl.ds(pl.multiple_of(j * blk, blk), blk)
        lo = 0 if diag is None else diag * blk
        st = jnp.dot(kx_ref[0, hd, rows, :], qt_ref[0, hd, :, lo:], preferred_element_type=F32)
        if diag is not None:
            st = jnp.where(causal[:, :Q_TILE - lo], st, -jnp.inf)
        if lo:
            s_ref[hd, :, :lo] = jnp.full((blk, lo), -jnp.inf, F32)
            smax_ref[hd, :, :lo] = jnp.full((1, lo), -jnp.inf, F32)
        s_ref[hd, :, lo:] = st
        smax_ref[hd, :, lo:] = jnp.max(st, axis=0, keepdims=True)

    def step(j, j_next, next_diag=None, next_q_ref=q_ref):
        for hd in range(N_HEADS):
            st = s_ref[hd]
            m_old = m_ref[hd]
            m_new = jnp.maximum(m_old, smax_ref[hd])
            scores(j_next, hd, next_diag, next_q_ref)
            alpha = jnp.exp2(m_old - m_new)
            pt = jnp.exp2(st - m_new).astype(BF16)
            pv = jnp.dot(v_ref[0, hd, j], pt, preferred_element_type=F32)
            acc_ref[hd] = acc_ref[hd] * alpha + pv
            m_ref[hd] = m_new

    @pl.when(qi == 0)
    def _():
        for hd in range(N_HEADS):
            scores(n_off, hd, 0)

    for d in range(1, n_diag):
        step(n_off + d - 1, n_off + d, d)

    def group(i):
        j0 = i * n_diag
        step(jnp.where(i == 0, n_off + n_diag - 1, j0 - 1), j0)
        for d in range(1, n_diag):
            step(j0 + d - 1, j0 + d)

    def body(t, carry):
        for g in range(LOOP_GROUPS):
            group(t * LOOP_GROUPS + g)
        return carry

    n_body = qi // LOOP_GROUPS
    lax.fori_loop(0, n_body, body, 0)
    for g in range(LOOP_GROUPS - 1):
        @pl.when(n_body * LOOP_GROUPS + g < qi)
        def _():
            group(n_body * LOOP_GROUPS + g)

    last = jnp.where(n_off == 0, n_diag - 1, n_off - 1)
    step(last, jnp.minimum(n_off + n_diag, nblk - n_diag), 0, qnext_ref)

    for hd in range(N_HEADS):
        acc = acc_ref[hd]
        o_ref[0, hd * HEAD_DIM:(hd + 1) * HEAD_DIM, :] = (
            acc[:HEAD_DIM] / acc[HEAD_DIM:HEAD_DIM + 1]).astype(BF16)


def _attn(q, kx, v, *, batch, seq):
    nblk = seq // ATTN_BLOCK
    n_tiles = seq // Q_TILE
    q_spec = lambda tile: pl.BlockSpec((1, N_HEADS, QK_DEPTH, Q_TILE),
                                       lambda b, i: (b, 0, 0, tile(i)))
    return pl.pallas_call(
        functools.partial(_attn_kernel, nblk=nblk),
        grid=(batch, n_tiles),
        in_specs=[q_spec(lambda i: i), q_spec(lambda i: jnp.minimum(i + 1, n_tiles - 1)),
                  pl.BlockSpec((1, N_HEADS, seq, LANES), lambda b, i: (b, 0, 0, 0)),
                  pl.BlockSpec((1, N_HEADS, nblk, V_ROWS, ATTN_BLOCK), lambda b, i: (b, 0, 0, 0, 0))],
        out_specs=pl.BlockSpec((1, ATTN_WIDTH, Q_TILE), lambda b, i: (b, 0, i)),
        out_shape=jax.ShapeDtypeStruct((batch, ATTN_WIDTH, seq), BF16),
        scratch_shapes=[pltpu.VMEM((N_HEADS, ATTN_BLOCK, Q_TILE), F32),
                        pltpu.VMEM((N_HEADS, 1, Q_TILE), F32),
                        pltpu.VMEM((N_HEADS, 1, Q_TILE), F32),
                        pltpu.VMEM((N_HEADS, V_ROWS, Q_TILE), F32)],
        compiler_params=pltpu.CompilerParams(dimension_semantics=("arbitrary", "arbitrary"),
                                             vmem_limit_bytes=VMEM_LIMIT),
        name="attn",
    )(q, q, kx, v)


def _merge_kernel(h_ref, at_ref, sgu_ref, gain_ref, wgatet_ref, bgate_ref, wua_ref, wug_ref,
                  wout_ref, o_ref):
    h = h_ref[...]
    nb = _rms_norm(h, gain_ref[...]).astype(BF16)
    logits = lax.dot_general(nb, wgatet_ref[...], (((1,), (1,)), ((), ())),
                             preferred_element_type=F32)
    gates = jax.nn.sigmoid(logits + bgate_ref[...])
    up_a = lax.dot_general(at_ref[0], wua_ref[...], (((0,), (0,)), ((), ())),
                           preferred_element_type=F32)
    up_g = jnp.dot(sgu_ref[...], wug_ref[...], preferred_element_type=F32)
    merged = gates[:, :D_MODEL] * up_a + gates[:, D_MODEL:] * up_g
    o_ref[...] = h + jnp.dot(merged.astype(BF16), wout_ref[...], preferred_element_type=F32)


def _merge(h, at, sgu, gain, wgate, bgate, wua, wug, wout, *, seq):
    t = h.shape[0]
    tiles_per_seq = seq // TOKEN_TILE
    tok = lambda i: (i, 0)
    return pl.pallas_call(
        _merge_kernel,
        grid=(t // TOKEN_TILE,),
        in_specs=[pl.BlockSpec((TOKEN_TILE, D_MODEL), tok),
                  pl.BlockSpec((1, ATTN_WIDTH, TOKEN_TILE),
                               lambda i: (i // tiles_per_seq, 0, i % tiles_per_seq)),
                  pl.BlockSpec((TOKEN_TILE, SGU_WIDTH), tok),
                  _resident((1, D_MODEL)), _resident(wgate.shape), _resident(bgate.shape),
                  _resident(wua.shape), _resident(wug.shape), _resident(wout.shape)],
        out_specs=pl.BlockSpec((TOKEN_TILE, D_MODEL), tok),
        out_shape=jax.ShapeDtypeStruct((t, D_MODEL), F32),
        compiler_params=_params(),
        name="merge",
    )(h, at, sgu, gain, wgate, bgate, wua, wug, wout)


def kernel(x, ffn1_norm, ffn1_w_in, ffn1_w_out, mix_norm, w_in, b_forget, b_gate, sgu_ln_gain,
           sgu_ln_bias, sgu_w_spatial, sgu_b_spatial, w_up_attn, w_up_sgu, w_out, ffn2_norm,
           ffn2_w_in, ffn2_w_out, final_norm):
    batch, seq, _ = x.shape
    depth = ffn1_norm.shape[0]
    row = lambda v: v.reshape(1, -1)
    fgain = row(final_norm)
    h = x.reshape(batch * seq, D_MODEL)
    for l in range(depth):
        last = l == depth - 1
        h = _ffn(h, row(ffn1_norm[l]), ffn1_w_in[l], ffn1_w_out[l], fgain, final_norm=False)

        o_k, o_v, o_f = ATTN_WIDTH, 2 * ATTN_WIDTH, 3 * ATTN_WIDTH
        o_u = o_f + N_HEADS
        o_vg, o_g = o_u + SGU_WIDTH, o_u + 2 * SGU_WIDTH
        wt = w_in[l].T
        wqt = (wt[:o_k] * (HEAD_DIM ** -0.5)).astype(BF16)
        wvt = wt[o_v:o_f].astype(BF16)
        wft = jnp.repeat(wt[o_f:o_u], N_SPLIT, axis=0)
        wft = jnp.pad(wft, ((0, LANES - N_SPLIT * N_HEADS), (0, 0)))
        wnt = jnp.concatenate([wt[o_k:o_v], wt[o_u:o_g], wft], axis=0).astype(BF16)
        bf = jnp.pad(jnp.repeat(b_forget[l], N_SPLIT), (0, LANES - N_SPLIT * N_HEADS)).reshape(1, LANES)
        wsp = sgu_w_spatial[l].reshape(N_GROUPS // SGU_PACK, SGU_PACK, CHUNK, CHUNK)
        wsp = wsp.transpose(0, 2, 1, 3).reshape(N_GROUPS // SGU_PACK, CHUNK, SGU_PACK * CHUNK)
        bsp = jnp.repeat(sgu_b_spatial[l].T, GROUP_DIM, axis=1)

        q, kx, v, sgu = _proj(h, row(mix_norm[l]), wqt, wvt, wnt, bf, row(sgu_ln_gain[l]),
                              row(sgu_ln_bias[l]), wsp, bsp, batch=batch, seq=seq)
        at = _attn(q, kx, v, batch=batch, seq=seq)
        h = _merge(h, at, sgu, row(mix_norm[l]), wt[o_g:].astype(BF16), row(b_gate[l]),
                   w_up_attn[l].astype(BF16), w_up_sgu[l].astype(BF16), w_out[l].astype(BF16),
                   seq=seq)
        h = _ffn(h, row(ffn2_norm[l]), ffn2_w_in[l], ffn2_w_out[l], fgain, final_norm=last)
    return h.reshape(batch, seq, D_MODEL)
```

```python
import functools

import jax
import jax.numpy as jnp
from jax import lax
from jax.experimental import pallas as pl
from jax.experimental.pallas import tpu as pltpu

F32 = jnp.float32
BF16 = jnp.bfloat16

D_MODEL = 1024
D_FF = 2816
N_HEADS = 8
HEAD_DIM = 64
ATTN_WIDTH = N_HEADS * HEAD_DIM
N_GROUPS = 8
GROUP_DIM = 64
SGU_WIDTH = N_GROUPS * GROUP_DIM
CHUNK = 128
RMS_EPS = 1e-6
LN_EPS = 1e-5
LOG2E = 1.4426950408889634

LANES = 128
BF16_SUBLANES = 16
V7X_VMEM_BYTES = 64 * 1024 * 1024
VMEM_LIMIT = V7X_VMEM_BYTES * 7 // 8

TOKEN_TILE = 512
PROJ_TILE = 1024
MERGE_TILE = 1024
FFN2_TILE = 1024
FFN_COLS = 256
ATTN_BLOCK = 256
Q_TILE = 512
LOOP_GROUPS = 2
V_ROWS = HEAD_DIM + 16
N_SPLIT = 3
QK_DEPTH = LANES
SGU_PACK = 4


def _rms_norm(x, gain):
    return (x * lax.rsqrt(jnp.mean(x * x, axis=-1, keepdims=True) + RMS_EPS)) * gain


def _gelu(x):
    return x * (lax.erf(x * (2.0 ** -0.5)) + 1.0) * 0.5


def _resident(shape):
    zeros = (0,) * len(shape)
    return pl.BlockSpec(shape, lambda *_: zeros, pipeline_mode=pl.Buffered(1))


def _params():
    return pltpu.CompilerParams(dimension_semantics=("arbitrary",),
                                vmem_limit_bytes=VMEM_LIMIT)


def _ffn_kernel(*refs, final_norm, tile, n_cast):
    x_ref, gain_ref, wi_ref, wo_ref, fgain_ref = refs[:5]
    cast_src = refs[5:5 + n_cast]
    o_ref = refs[5 + n_cast]
    cast_dst = refs[6 + n_cast:6 + 2 * n_cast]
    act_ref = refs[6 + 2 * n_cast]

    for src, dst in zip(cast_src, cast_dst):
        dst[...] = src[...].astype(dst.dtype)

    wdt = wi_ref.dtype
    for s in range(tile // TOKEN_TILE):
        rows = slice(s * TOKEN_TILE, (s + 1) * TOKEN_TILE)
        x = x_ref[rows, :]
        n = _rms_norm(x, gain_ref[...]).astype(wdt)
        for j in range(D_FF // FFN_COLS):
            lo = j * FFN_COLS
            gate = jnp.dot(n, wi_ref[:, lo:lo + FFN_COLS], preferred_element_type=F32)
            up = jnp.dot(n, wi_ref[:, D_FF + lo:D_FF + lo + FFN_COLS], preferred_element_type=F32)
            act_ref[s, :, lo:lo + FFN_COLS] = (gate * jax.nn.sigmoid(gate) * up).astype(wdt)
        y = jnp.dot(act_ref[s], wo_ref[...], preferred_element_type=F32)
        h = x + 0.5 * y
        if final_norm:
            h = _rms_norm(h, fgain_ref[...])
        o_ref[rows, :] = h


def _cast_slab_rows(n_rows, n_steps):
    for rows in range(BF16_SUBLANES, n_rows + 1, BF16_SUBLANES):
        if n_rows % rows == 0 and n_rows // rows <= n_steps:
            return rows
    raise ValueError((n_rows, n_steps))


def _ffn(x, gain, wi, wo, fgain, *, final_norm, tile, cast=()):
    t = x.shape[0]
    steps = t // tile
    tok = pl.BlockSpec((tile, D_MODEL), lambda i: (i, 0))
    slab_specs = []
    for arr in cast:
        rows = _cast_slab_rows(arr.shape[0], steps)
        last = arr.shape[0] // rows - 1
        slab_specs.append(pl.BlockSpec((rows, arr.shape[1]),
                                       lambda i, last=last: (jnp.minimum(i, last), 0)))
    outs = pl.pallas_call(
        functools.partial(_ffn_kernel, final_norm=final_norm, tile=tile, n_cast=len(cast)),
        grid=(steps,),
        in_specs=[tok, _resident((1, D_MODEL)), _resident(wi.shape), _resident(wo.shape),
                  _resident((1, D_MODEL))] + slab_specs,
        out_specs=[tok] + slab_specs,
        out_shape=[jax.ShapeDtypeStruct((t, D_MODEL), F32)]
                  + [jax.ShapeDtypeStruct(arr.shape, BF16) for arr in cast],
        scratch_shapes=[pltpu.VMEM((tile // TOKEN_TILE, TOKEN_TILE, D_FF), wi.dtype)],
        compiler_params=_params(),
        name="ffn_final" if final_norm else "ffn",
    )(x, gain, wi, wo, fgain, *cast)
    return outs[0], tuple(outs[1:])


def _split_bf16(x):
    parts = []
    for _ in range(N_SPLIT):
        p = x.astype(BF16).astype(F32)
        parts.append(p)
        x = x - p
    return parts


def _proj_kernel(h_ref, gain_ref, wqt_ref, wvt_ref, wnt_ref, bf_ref, lng_ref, lnb_ref,
                 wsp_ref, bsp_ref,
                 q_ref, kx_ref, v_ref, sgu_ref, carry_ref, *, tiles_per_seq):
    i = pl.program_id(0)

    @pl.when(i % tiles_per_seq == 0)
    def _():
        carry_ref[...] = jnp.zeros_like(carry_ref)

    tdims = (((1,), (1,)), ((), ()))
    sub = ATTN_BLOCK
    sel_row = lax.broadcasted_iota(jnp.int32, (QK_DEPTH - HEAD_DIM, sub), 0)
    ones_rows = jnp.where(sel_row < N_SPLIT, 1.0, 0.0).astype(BF16)
    pad_row = lax.broadcasted_iota(jnp.int32, (V_ROWS - HEAD_DIM, sub), 0)
    ones_row = jnp.where(pad_row == 0, 1.0, 0.0).astype(BF16)
    lane = lax.broadcasted_iota(jnp.int32, (CHUNK, LANES), 1)
    term = lane % N_SPLIT
    live = lane < N_SPLIT * N_HEADS

    def by_term(parts):
        out = parts[-1]
        for j in range(N_SPLIT - 2, -1, -1):
            out = jnp.where(term == j, parts[j], out)
        return out

    row = lax.broadcasted_iota(jnp.int32, (CHUNK, CHUNK), 0)
    col = lax.broadcasted_iota(jnp.int32, (CHUNK, CHUNK), 1)
    tri = jnp.where(row >= col, 1.0, 0.0).astype(BF16)
    pack_w = SGU_PACK * GROUP_DIM
    wrow = lax.broadcasted_iota(jnp.int32, (CHUNK, SGU_PACK * CHUNK), 0)
    wcol = lax.broadcasted_iota(jnp.int32, (CHUNK, SGU_PACK * CHUNK), 1) % CHUNK
    wsp = [jnp.where(wcol <= wrow, wsp_ref[p], 0.0).astype(BF16)
           for p in range(N_GROUPS // SGU_PACK)]
    lane_group = lax.broadcasted_iota(jnp.int32, (CHUNK, pack_w), 1) // GROUP_DIM

    def project(s):
        r0 = s * sub
        nb = _rms_norm(h_ref[r0:r0 + sub, :], gain_ref[...]).astype(BF16)
        qt = lax.dot_general(wqt_ref[...], nb, tdims, preferred_element_type=F32)
        qt = (qt * LOG2E).astype(BF16)
        vt = lax.dot_general(wvt_ref[...], nb, tdims, preferred_element_type=F32).astype(BF16)
        for hd in range(N_HEADS):
            q_ref[0, hd, :HEAD_DIM, r0:r0 + sub] = qt[hd * HEAD_DIM:(hd + 1) * HEAD_DIM, :]
            q_ref[0, hd, HEAD_DIM:, r0:r0 + sub] = ones_rows
            v_ref[0, hd, s, :HEAD_DIM, :] = vt[hd * HEAD_DIM:(hd + 1) * HEAD_DIM, :]
            v_ref[0, hd, s, HEAD_DIM:, :] = ones_row
        return lax.dot_general(nb, wnt_ref[...], tdims, preferred_element_type=F32)

    def tail(s, z, carry):
        r0 = s * sub
        zk = z[:, :ATTN_WIDTH]
        u = z[:, ATTN_WIDTH:ATTN_WIDTH + SGU_WIDTH]
        vg = z[:, ATTN_WIDTH + SGU_WIDTH:ATTN_WIDTH + 2 * SGU_WIDTH]
        f = z[:, ATTN_WIDTH + 2 * SGU_WIDTH:] + bf_ref[...]
        logf = jnp.minimum(f, 0.0) - jnp.log1p(jnp.exp(-jnp.abs(f)))

        gu = _gelu(u)
        gv = _gelu(vg)
        mu = jnp.mean(gv, axis=-1, keepdims=True)
        dv = gv - mu
        var = jnp.mean(dv * dv, axis=-1, keepdims=True)
        vn = ((dv * lax.rsqrt(var + LN_EPS)) * lng_ref[...] + lnb_ref[...]).astype(BF16)

        for c in range(sub // CHUNK):
            rows = slice(c * CHUNK, (c + 1) * CHUNK)
            out_rows = slice(r0 + c * CHUNK, r0 + (c + 1) * CHUNK)
            local = jnp.dot(tri, by_term(_split_bf16(logf[rows])).astype(BF16),
                            preferred_element_type=F32)
            shifted = {0: local}
            for d in range(1, N_SPLIT):
                shifted[d] = pltpu.roll(local, d, 1)
                shifted[-d] = pltpu.roll(local, LANES - d, 1)
            csum = carry
            for j in range(N_SPLIT):
                csum = csum + by_term([shifted[t - j] for t in range(N_SPLIT)])
            carry = csum[CHUNK - 1:, :]
            bias = jnp.where(live, by_term(_split_bf16(csum * -LOG2E)), 0.0)
            for hd in range(N_HEADS):
                pair = hd // 2
                kslab = zk[rows, pair * LANES:(pair + 1) * LANES]
                if hd % 2:
                    kslab = pltpu.roll(kslab, HEAD_DIM, 1)
                hbias = pltpu.roll(bias, HEAD_DIM - N_SPLIT * hd, 1)
                kx = jnp.where(lane < HEAD_DIM, kslab,
                               jnp.where(lane < HEAD_DIM + N_SPLIT, hbias, 0.0))
                kx_ref[0, hd, out_rows, :] = kx.astype(BF16)

            mixed = []
            for p in range(N_GROUPS // SGU_PACK):
                slab = vn[rows, p * pack_w:(p + 1) * pack_w]
                blocks = jnp.concatenate(
                    [jnp.where(lane_group == r, slab, jnp.zeros_like(slab))
                     for r in range(SGU_PACK)], axis=0)
                mixed.append(jnp.dot(wsp[p], blocks, preferred_element_type=F32))
            mixed = jnp.concatenate(mixed, axis=1) + bsp_ref[...]
            sgu_ref[out_rows, :] = (gu[rows] * mixed).astype(BF16)
        return carry

    n_sub = PROJ_TILE // sub
    carry = carry_ref[...]
    z_prev = project(0)
    for s in range(1, n_sub):
        z = project(s)
        carry = tail(s - 1, z_prev, carry)
        z_prev = z
    carry_ref[...] = tail(n_sub - 1, z_prev, carry)


def _proj(h, gain, wqt, wvt, wnt, bf, lng, lnb, wsp, bsp, *, batch, seq):
    t = batch * seq
    tiles_per_seq = seq // PROJ_TILE
    sub = PROJ_TILE // ATTN_BLOCK
    tok = lambda i: (i, 0)
    return pl.pallas_call(
        functools.partial(_proj_kernel, tiles_per_seq=tiles_per_seq),
        grid=(t // PROJ_TILE,),
        in_specs=[pl.BlockSpec((PROJ_TILE, D_MODEL), tok), _resident((1, D_MODEL)),
                  _resident(wqt.shape), _resident(wvt.shape), _resident(wnt.shape),
                  _resident((1, LANES)), _resident((1, SGU_WIDTH)), _resident((1, SGU_WIDTH)),
                  _resident(wsp.shape), _resident(bsp.shape)],
        out_specs=[pl.BlockSpec((1, N_HEADS, QK_DEPTH, PROJ_TILE),
                                lambda i: (i // tiles_per_seq, 0, 0, i % tiles_per_seq)),
                   pl.BlockSpec((1, N_HEADS, PROJ_TILE, LANES),
                                lambda i: (i // tiles_per_seq, 0, i % tiles_per_seq, 0)),
                   pl.BlockSpec((1, N_HEADS, sub, V_ROWS, ATTN_BLOCK),
                                lambda i: (i // tiles_per_seq, 0, i % tiles_per_seq, 0, 0)),
                   pl.BlockSpec((PROJ_TILE, SGU_WIDTH), tok)],
        out_shape=[jax.ShapeDtypeStruct((batch, N_HEADS, QK_DEPTH, seq), BF16),
                   jax.ShapeDtypeStruct((batch, N_HEADS, seq, LANES), BF16),
                   jax.ShapeDtypeStruct((batch, N_HEADS, seq // ATTN_BLOCK, V_ROWS, ATTN_BLOCK), BF16),
                   jax.ShapeDtypeStruct((t, SGU_WIDTH), BF16)],
        scratch_shapes=[pltpu.VMEM((1, LANES), F32)],
        compiler_params=_params(),
        name="proj",
    )(h, gain, wqt, wvt, wnt, bf, lng, lnb, wsp, bsp)


def _attn_kernel(q_ref, qnext_ref, kx_ref, v_ref, o_ref, s_ref, smax_ref, m_ref, acc_ref, *, nblk):
    blk = ATTN_BLOCK
    n_diag = Q_TILE // blk
    qi = pl.program_id(1)
    n_off = qi * n_diag

    m_ref[...] = jnp.full(m_ref.shape, -jnp.inf, F32)
    acc_ref[...] = jnp.zeros(acc_ref.shape, F32)

    key_pos = lax.broadcasted_iota(jnp.int32, (blk, Q_TILE), 0)
    qry_pos = lax.broadcasted_iota(jnp.int32, (blk, Q_TILE), 1)
    causal = key_pos <= qry_pos

    def scores(j, hd, diag, qt_ref=q_ref):
        rows = pl.ds(pl.multiple_of(j * blk, blk), blk)
        lo = 0 if diag is None else diag * blk
        st = jnp.dot(kx_ref[0, hd, rows, :], qt_ref[0, hd, :, lo:], preferred_element_type=F32)
        if diag is not None:
            st = jnp.where(causal[:, :Q_TILE - lo], st, -jnp.inf)
        if lo:
            s_ref[hd, :, :lo] = jnp.full((blk, lo), -jnp.inf, F32)
            smax_ref[hd, :, :lo] = jnp.full((1, lo), -jnp.inf, F32)
        s_ref[hd, :, lo:] = st
        smax_ref[hd, :, lo:] = jnp.max(st, axis=0, keepdims=True)

    def step(j, j_next, next_diag=None, next_q_ref=q_ref):
        for hd in range(N_HEADS):
            st = s_ref[hd]
            m_old = m_ref[hd]
            m_new = jnp.maximum(m_old, smax_ref[hd])
            scores(j_next, hd, next_diag, next_q_ref)
            alpha = jnp.exp2(m_old - m_new)
            pt = jnp.exp2(st - m_new).astype(BF16)
            pv = jnp.dot(v_ref[0, hd, j], pt, preferred_element_type=F32)
            acc_ref[hd] = acc_ref[hd] * alpha + pv
            m_ref[hd] = m_new

    @pl.when(qi == 0)
    def _():
        for hd in range(N_HEADS):
            scores(n_off, hd, 0)

    for d in range(1, n_diag):
        step(n_off + d - 1, n_off + d, d)

    def group(i):
        j0 = i * n_diag
        step(jnp.where(i == 0, n_off + n_diag - 1, j0 - 1), j0)
        for d in range(1, n_diag):
            step(j0 + d - 1, j0 + d)

    def body(t, carry):
        for g in range(LOOP_GROUPS):
            group(t * LOOP_GROUPS + g)
        return carry

    n_body = qi // LOOP_GROUPS
    lax.fori_loop(0, n_body, body, 0)
    for g in range(LOOP_GROUPS - 1):
        @pl.when(n_body * LOOP_GROUPS + g < qi)
        def _():
            group(n_body * LOOP_GROUPS + g)

    last = jnp.where(n_off == 0, n_diag - 1, n_off - 1)
    step(last, jnp.minimum(n_off + n_diag, nblk - n_diag), 0, qnext_ref)

    for hd in range(N_HEADS):
        acc = acc_ref[hd]
        o_ref[0, hd * HEAD_DIM:(hd + 1) * HEAD_DIM, :] = (
            acc[:HEAD_DIM] / acc[HEAD_DIM:HEAD_DIM + 1]).astype(BF16)


def _attn(q, kx, v, *, batch, seq):
    nblk = seq // ATTN_BLOCK
    n_tiles = seq // Q_TILE
    q_spec = lambda tile: pl.BlockSpec((1, N_HEADS, QK_DEPTH, Q_TILE),
                                       lambda b, i: (b, 0, 0, tile(i)))
    return pl.pallas_call(
        functools.partial(_attn_kernel, nblk=nblk),
        grid=(batch, n_tiles),
        in_specs=[q_spec(lambda i: i), q_spec(lambda i: jnp.minimum(i + 1, n_tiles - 1)),
                  pl.BlockSpec((1, N_HEADS, seq, LANES), lambda b, i: (b, 0, 0, 0)),
                  pl.BlockSpec((1, N_HEADS, nblk, V_ROWS, ATTN_BLOCK), lambda b, i: (b, 0, 0, 0, 0))],
        out_specs=pl.BlockSpec((1, ATTN_WIDTH, Q_TILE), lambda b, i: (b, 0, i)),
        out_shape=jax.ShapeDtypeStruct((batch, ATTN_WIDTH, seq), BF16),
        scratch_shapes=[pltpu.VMEM((N_HEADS, ATTN_BLOCK, Q_TILE), F32),
                        pltpu.VMEM((N_HEADS, 1, Q_TILE), F32),
                        pltpu.VMEM((N_HEADS, 1, Q_TILE), F32),
                        pltpu.VMEM((N_HEADS, V_ROWS, Q_TILE), F32)],
        compiler_params=pltpu.CompilerParams(dimension_semantics=("arbitrary", "arbitrary"),
                                             vmem_limit_bytes=VMEM_LIMIT),
        name="attn",
    )(q, q, kx, v)


def _merge_kernel(h_ref, at_ref, sgu_ref, gain_ref, wgatet_ref, bgate_ref, wua_ref, wug_ref,
                  wout_ref, o_ref):
    for s in range(MERGE_TILE // TOKEN_TILE):
        rows = slice(s * TOKEN_TILE, (s + 1) * TOKEN_TILE)
        h = h_ref[rows, :]
        nb = _rms_norm(h, gain_ref[...]).astype(BF16)
        logits = lax.dot_general(nb, wgatet_ref[...], (((1,), (1,)), ((), ())),
                                 preferred_element_type=F32)
        gates = jax.nn.sigmoid(logits + bgate_ref[...])
        up_a = lax.dot_general(at_ref[0, :, rows], wua_ref[...], (((0,), (0,)), ((), ())),
                               preferred_element_type=F32)
        up_g = jnp.dot(sgu_ref[rows, :], wug_ref[...], preferred_element_type=F32)
        merged = gates[:, :D_MODEL] * up_a + gates[:, D_MODEL:] * up_g
        o_ref[rows, :] = h + jnp.dot(merged.astype(BF16), wout_ref[...],
                                     preferred_element_type=F32)


def _merge(h, at, sgu, gain, wgate, bgate, wua, wug, wout, *, seq):
    t = h.shape[0]
    tiles_per_seq = seq // MERGE_TILE
    tok = lambda i: (i, 0)
    return pl.pallas_call(
        _merge_kernel,
        grid=(t // MERGE_TILE,),
        in_specs=[pl.BlockSpec((MERGE_TILE, D_MODEL), tok),
                  pl.BlockSpec((1, ATTN_WIDTH, MERGE_TILE),
                               lambda i: (i // tiles_per_seq, 0, i % tiles_per_seq)),
                  pl.BlockSpec((MERGE_TILE, SGU_WIDTH), tok),
                  _resident((1, D_MODEL)), _resident(wgate.shape), _resident(bgate.shape),
                  _resident(wua.shape), _resident(wug.shape), _resident(wout.shape)],
        out_specs=pl.BlockSpec((MERGE_TILE, D_MODEL), tok),
        out_shape=jax.ShapeDtypeStruct((t, D_MODEL), F32),
        compiler_params=_params(),
        name="merge",
    )(h, at, sgu, gain, wgate, bgate, wua, wug, wout)


def kernel(x, ffn1_norm, ffn1_w_in, ffn1_w_out, mix_norm, w_in, b_forget, b_gate, sgu_ln_gain,
           sgu_ln_bias, sgu_w_spatial, sgu_b_spatial, w_up_attn, w_up_sgu, w_out, ffn2_norm,
           ffn2_w_in, ffn2_w_out, final_norm):
    batch, seq, _ = x.shape
    depth = ffn1_norm.shape[0]
    row = lambda v: v.reshape(1, -1)
    fgain = row(final_norm)
    h = x.reshape(batch * seq, D_MODEL)
    for l in range(depth):
        last = l == depth - 1
        h, (wi2, wo2) = _ffn(h, row(ffn1_norm[l]), ffn1_w_in[l], ffn1_w_out[l], fgain,
                             final_norm=False, tile=TOKEN_TILE,
                             cast=(ffn2_w_in[l], ffn2_w_out[l]))

        o_k, o_v, o_f = ATTN_WIDTH, 2 * ATTN_WIDTH, 3 * ATTN_WIDTH
        o_u = o_f + N_HEADS
        o_vg, o_g = o_u + SGU_WIDTH, o_u + 2 * SGU_WIDTH
        wt = w_in[l].T
        wqt = (wt[:o_k] * (HEAD_DIM ** -0.5)).astype(BF16)
        wvt = wt[o_v:o_f].astype(BF16)
        wft = jnp.repeat(wt[o_f:o_u], N_SPLIT, axis=0)
        wft = jnp.pad(wft, ((0, LANES - N_SPLIT * N_HEADS), (0, 0)))
        wnt = jnp.concatenate([wt[o_k:o_v], wt[o_u:o_g], wft], axis=0).astype(BF16)
        bf = jnp.pad(jnp.repeat(b_forget[l], N_SPLIT), (0, LANES - N_SPLIT * N_HEADS)).reshape(1, LANES)
        wsp = sgu_w_spatial[l].reshape(N_GROUPS // SGU_PACK, SGU_PACK, CHUNK, CHUNK)
        wsp = wsp.transpose(0, 2, 1, 3).reshape(N_GROUPS // SGU_PACK, CHUNK, SGU_PACK * CHUNK)
        bsp = jnp.repeat(sgu_b_spatial[l].T, GROUP_DIM, axis=1)

        q, kx, v, sgu = _proj(h, row(mix_norm[l]), wqt, wvt, wnt, bf, row(sgu_ln_gain[l]),
                              row(sgu_ln_bias[l]), wsp, bsp, batch=batch, seq=seq)
        at = _attn(q, kx, v, batch=batch, seq=seq)
        h = _merge(h, at, sgu, row(mix_norm[l]), wt[o_g:].astype(BF16), row(b_gate[l]),
                   w_up_attn[l].astype(BF16), w_up_sgu[l].astype(BF16), w_out[l].astype(BF16),
                   seq=seq)
        h, _ = _ffn(h, row(ffn2_norm[l]), wi2, wo2, fgain, final_norm=last, tile=FFN2_TILE)
    return h.reshape(batch, seq, D_MODEL)
```

```python
import functools

import jax
import jax.numpy as jnp
from jax import lax
from jax.experimental import pallas as pl
from jax.experimental.pallas import tpu as pltpu

F32 = jnp.float32
BF16 = jnp.bfloat16

D_MODEL = 1024
D_FF = 2816
N_HEADS = 8
HEAD_DIM = 64
ATTN_WIDTH = N_HEADS * HEAD_DIM
N_GROUPS = 8
GROUP_DIM = 64
SGU_WIDTH = N_GROUPS * GROUP_DIM
CHUNK = 128
RMS_EPS = 1e-6
LN_EPS = 1e-5
LOG2E = 1.4426950408889634

LANES = 128
BF16_SUBLANES = 16
V7X_VMEM_BYTES = 64 * 1024 * 1024
VMEM_LIMIT = V7X_VMEM_BYTES * 7 // 8

TOKEN_TILE = 512
PROJ_TILE = 1024
MERGE_TILE = 1024
FFN2_TILE = 1024
FFN_COLS = 256
ATTN_BLOCK = 256
Q_TILE = 512
LOOP_GROUPS = 2
V_ROWS = HEAD_DIM + 16
N_SPLIT = 3
QK_DEPTH = LANES
SGU_PACK = 4


def _rms_norm(x, gain):
    return (x * lax.rsqrt(jnp.mean(x * x, axis=-1, keepdims=True) + RMS_EPS)) * gain


def _gelu(x):
    return x * (lax.erf(x * (2.0 ** -0.5)) + 1.0) * 0.5


def _resident(shape):
    zeros = (0,) * len(shape)
    return pl.BlockSpec(shape, lambda *_: zeros, pipeline_mode=pl.Buffered(1))


def _params():
    return pltpu.CompilerParams(dimension_semantics=("arbitrary",),
                                vmem_limit_bytes=VMEM_LIMIT)


def _ffn_kernel(*refs, final_norm, tile, n_cast):
    x_ref, gain_ref, wi_ref, wo_ref, fgain_ref = refs[:5]
    cast_src = refs[5:5 + n_cast]
    o_ref = refs[5 + n_cast]
    cast_dst = refs[6 + n_cast:6 + 2 * n_cast]
    act_ref = refs[6 + 2 * n_cast]

    for src, dst in zip(cast_src, cast_dst):
        dst[...] = src[...].astype(dst.dtype)

    wdt = wi_ref.dtype
    for s in range(tile // TOKEN_TILE):
        rows = slice(s * TOKEN_TILE, (s + 1) * TOKEN_TILE)
        x = x_ref[rows, :]
        n = _rms_norm(x, gain_ref[...]).astype(wdt)
        for j in range(D_FF // FFN_COLS):
            lo = j * FFN_COLS
            gate = jnp.dot(n, wi_ref[:, lo:lo + FFN_COLS], preferred_element_type=F32)
            up = jnp.dot(n, wi_ref[:, D_FF + lo:D_FF + lo + FFN_COLS], preferred_element_type=F32)
            act_ref[s, :, lo:lo + FFN_COLS] = (gate * jax.nn.sigmoid(gate) * up).astype(wdt)
        y = jnp.dot(act_ref[s], wo_ref[...], preferred_element_type=F32)
        h = x + 0.5 * y
        if final_norm:
            h = _rms_norm(h, fgain_ref[...])
        o_ref[rows, :] = h


def _cast_slab_rows(n_rows, n_steps):
    for rows in range(BF16_SUBLANES, n_rows + 1, BF16_SUBLANES):
        if n_rows % rows == 0 and n_rows // rows <= n_steps:
            return rows
    raise ValueError((n_rows, n_steps))


def _ffn(x, gain, wi, wo, fgain, *, final_norm, tile, cast=()):
    t = x.shape[0]
    steps = t // tile
    tok = pl.BlockSpec((tile, D_MODEL), lambda i: (i, 0))
    slab_specs = []
    for arr in cast:
        rows = _cast_slab_rows(arr.shape[0], steps)
        last = arr.shape[0] // rows - 1
        slab_specs.append(pl.BlockSpec((rows, arr.shape[1]),
                                       lambda i, last=last: (jnp.minimum(i, last), 0)))
    outs = pl.pallas_call(
        functools.partial(_ffn_kernel, final_norm=final_norm, tile=tile, n_cast=len(cast)),
        grid=(steps,),
        in_specs=[tok, _resident((1, D_MODEL)), _resident(wi.shape), _resident(wo.shape),
                  _resident((1, D_MODEL))] + slab_specs,
        out_specs=[tok] + slab_specs,
        out_shape=[jax.ShapeDtypeStruct((t, D_MODEL), F32)]
                  + [jax.ShapeDtypeStruct(arr.shape, BF16) for arr in cast],
        scratch_shapes=[pltpu.VMEM((tile // TOKEN_TILE, TOKEN_TILE, D_FF), wi.dtype)],
        compiler_params=_params(),
        name="ffn_final" if final_norm else "ffn",
    )(x, gain, wi, wo, fgain, *cast)
    return outs[0], tuple(outs[1:])


def _split_bf16(x):
    parts = []
    for _ in range(N_SPLIT):
        p = x.astype(BF16).astype(F32)
        parts.append(p)
        x = x - p
    return parts


def _proj_kernel(h_ref, gain_ref, wqt_ref, wvt_ref, wnt_ref, bf_ref, lng_ref, lnb_ref,
                 wsp_ref, bsp_ref,
                 q_ref, kx_ref, v_ref, sgu_ref, carry_ref, *, tiles_per_seq):
    i = pl.program_id(0)

    @pl.when(i % tiles_per_seq == 0)
    def _():
        carry_ref[...] = jnp.zeros_like(carry_ref)

    tdims = (((1,), (1,)), ((), ()))
    sub = ATTN_BLOCK
    sel_row = lax.broadcasted_iota(jnp.int32, (QK_DEPTH - HEAD_DIM, sub), 0)
    ones_rows = jnp.where(sel_row < N_SPLIT, 1.0, 0.0).astype(BF16)
    pad_row = lax.broadcasted_iota(jnp.int32, (V_ROWS - HEAD_DIM, sub), 0)
    ones_row = jnp.where(pad_row == 0, 1.0, 0.0).astype(BF16)
    lane = lax.broadcasted_iota(jnp.int32, (CHUNK, LANES), 1)
    term = lane % N_SPLIT
    live = lane < N_SPLIT * N_HEADS

    def by_term(parts):
        out = parts[-1]
        for j in range(N_SPLIT - 2, -1, -1):
            out = jnp.where(term == j, parts[j], out)
        return out

    row = lax.broadcasted_iota(jnp.int32, (CHUNK, CHUNK), 0)
    col = lax.broadcasted_iota(jnp.int32, (CHUNK, CHUNK), 1)
    tri = jnp.where(row >= col, 1.0, 0.0).astype(BF16)
    pack_w = SGU_PACK * GROUP_DIM
    wrow = lax.broadcasted_iota(jnp.int32, (CHUNK, SGU_PACK * CHUNK), 0)
    wcol = lax.broadcasted_iota(jnp.int32, (CHUNK, SGU_PACK * CHUNK), 1) % CHUNK
    wsp = [jnp.where(wcol <= wrow, wsp_ref[p], 0.0).astype(BF16)
           for p in range(N_GROUPS // SGU_PACK)]
    lane_group = lax.broadcasted_iota(jnp.int32, (CHUNK, pack_w), 1) // GROUP_DIM

    def project(s):
        r0 = s * sub
        nb = _rms_norm(h_ref[r0:r0 + sub, :], gain_ref[...]).astype(BF16)
        qt = lax.dot_general(wqt_ref[...], nb, tdims, preferred_element_type=F32)
        qt = (qt * LOG2E).astype(BF16)
        vt = lax.dot_general(wvt_ref[...], nb, tdims, preferred_element_type=F32).astype(BF16)
        for hd in range(N_HEADS):
            q_ref[0, hd, :HEAD_DIM, r0:r0 + sub] = qt[hd * HEAD_DIM:(hd + 1) * HEAD_DIM, :]
            q_ref[0, hd, HEAD_DIM:, r0:r0 + sub] = ones_rows
            v_ref[0, hd, s, :HEAD_DIM, :] = vt[hd * HEAD_DIM:(hd + 1) * HEAD_DIM, :]
            v_ref[0, hd, s, HEAD_DIM:, :] = ones_row
        return lax.dot_general(nb, wnt_ref[...], tdims, preferred_element_type=F32)

    def tail(s, z, carry):
        r0 = s * sub
        zk = z[:, :ATTN_WIDTH]
        u = z[:, ATTN_WIDTH:ATTN_WIDTH + SGU_WIDTH]
        vg = z[:, ATTN_WIDTH + SGU_WIDTH:ATTN_WIDTH + 2 * SGU_WIDTH]
        f = z[:, ATTN_WIDTH + 2 * SGU_WIDTH:] + bf_ref[...]
        logf = jnp.minimum(f, 0.0) - jnp.log1p(jnp.exp(-jnp.abs(f)))

        gu = _gelu(u)
        gv = _gelu(vg)
        mu = jnp.mean(gv, axis=-1, keepdims=True)
        dv = gv - mu
        var = jnp.mean(dv * dv, axis=-1, keepdims=True)
        vn = ((dv * lax.rsqrt(var + LN_EPS)) * lng_ref[...] + lnb_ref[...]).astype(BF16)

        for c in range(sub // CHUNK):
            rows = slice(c * CHUNK, (c + 1) * CHUNK)
            out_rows = slice(r0 + c * CHUNK, r0 + (c + 1) * CHUNK)
            local = jnp.dot(tri, by_term(_split_bf16(logf[rows])).astype(BF16),
                            preferred_element_type=F32)
            shifted = {0: local}
            for d in range(1, N_SPLIT):
                shifted[d] = pltpu.roll(local, d, 1)
                shifted[-d] = pltpu.roll(local, LANES - d, 1)
            csum = carry
            for j in range(N_SPLIT):
                csum = csum + by_term([shifted[t - j] for t in range(N_SPLIT)])
            carry = csum[CHUNK - 1:, :]
            bias = jnp.where(live, by_term(_split_bf16(csum * -LOG2E)), 0.0)
            for hd in range(N_HEADS):
                pair = hd // 2
                kslab = zk[rows, pair * LANES:(pair + 1) * LANES]
                if hd % 2:
                    kslab = pltpu.roll(kslab, HEAD_DIM, 1)
                hbias = pltpu.roll(bias, HEAD_DIM - N_SPLIT * hd, 1)
                kx = jnp.where(lane < HEAD_DIM, kslab,
                               jnp.where(lane < HEAD_DIM + N_SPLIT, hbias, 0.0))
                kx_ref[0, hd, out_rows, :] = kx.astype(BF16)

            mixed = []
            for p in range(N_GROUPS // SGU_PACK):
                slab = vn[rows, p * pack_w:(p + 1) * pack_w]
                blocks = jnp.concatenate(
                    [jnp.where(lane_group == r, slab, jnp.zeros_like(slab))
                     for r in range(SGU_PACK)], axis=0)
                mixed.append(jnp.dot(wsp[p], blocks, preferred_element_type=F32))
            mixed = jnp.concatenate(mixed, axis=1) + bsp_ref[...]
            sgu_ref[out_rows, :] = (gu[rows] * mixed).astype(BF16)
        return carry

    n_sub = PROJ_TILE // sub
    carry = carry_ref[...]
    z_prev = project(0)
    for s in range(1, n_sub):
        z = project(s)
        carry = tail(s - 1, z_prev, carry)
        z_prev = z
    carry_ref[...] = tail(n_sub - 1, z_prev, carry)


def _proj(h, gain, wqt, wvt, wnt, bf, lng, lnb, wsp, bsp, *, batch, seq):
    t = batch * seq
    tiles_per_seq = seq // PROJ_TILE
    sub = PROJ_TILE // ATTN_BLOCK
    tok = lambda i: (i, 0)
    return pl.pallas_call(
        functools.partial(_proj_kernel, tiles_per_seq=tiles_per_seq),
        grid=(t // PROJ_TILE,),
        in_specs=[pl.BlockSpec((PROJ_TILE, D_MODEL), tok), _resident((1, D_MODEL)),
                  _resident(wqt.shape), _resident(wvt.shape), _resident(wnt.shape),
                  _resident((1, LANES)), _resident((1, SGU_WIDTH)), _resident((1, SGU_WIDTH)),
                  _resident(wsp.shape), _resident(bsp.shape)],
        out_specs=[pl.BlockSpec((1, N_HEADS, QK_DEPTH, PROJ_TILE),
                                lambda i: (i // tiles_per_seq, 0, 0, i % tiles_per_seq)),
                   pl.BlockSpec((1, N_HEADS, PROJ_TILE, LANES),
                                lambda i: (i // tiles_per_seq, 0, i % tiles_per_seq, 0)),
                   pl.BlockSpec((1, N_HEADS, sub, V_ROWS, ATTN_BLOCK),
                                lambda i: (i // tiles_per_seq, 0, i % tiles_per_seq, 0, 0)),
                   pl.BlockSpec((PROJ_TILE, SGU_WIDTH), tok)],
        out_shape=[jax.ShapeDtypeStruct((batch, N_HEADS, QK_DEPTH, seq), BF16),
                   jax.ShapeDtypeStruct((batch, N_HEADS, seq, LANES), BF16),
                   jax.ShapeDtypeStruct((batch, N_HEADS, seq // ATTN_BLOCK, V_ROWS, ATTN_BLOCK), BF16),
                   jax.ShapeDtypeStruct((t, SGU_WIDTH), BF16)],
        scratch_shapes=[pltpu.VMEM((1, LANES), F32)],
        compiler_params=_params(),
        name="proj",
    )(h, gain, wqt, wvt, wnt, bf, lng, lnb, wsp, bsp)


def _attn_kernel(q_ref, qnext_ref, kx_ref, v_ref, o_ref, s_ref, smax_ref, m_ref, acc_ref, *, nblk):
    blk = ATTN_BLOCK
    n_diag = Q_TILE // blk
    qi = pl.program_id(1)
    n_off = qi * n_diag

    m_ref[...] = jnp.full(m_ref.shape, -jnp.inf, F32)
    acc_ref[...] = jnp.zeros(acc_ref.shape, F32)

    key_pos = lax.broadcasted_iota(jnp.int32, (blk, Q_TILE), 0)
    qry_pos = lax.broadcasted_iota(jnp.int32, (blk, Q_TILE), 1)
    causal = key_pos <= qry_pos

    def scores(j, hd, diag, qt_ref=q_ref):
        rows = pl.ds(pl.multiple_of(j * blk, blk), blk)
        lo = 0 if diag is None else diag * blk
        st = jnp.dot(kx_ref[0, hd, rows, :], qt_ref[0, hd, :, lo:], preferred_element_type=F32)
        if diag is not None:
            st = jnp.where(causal[:, :Q_TILE - lo], st, -jnp.inf)
        if lo:
            s_ref[hd, :, :lo] = jnp.full((blk, lo), -jnp.inf, F32)
            smax_ref[hd, :, :lo] = jnp.full((1, lo), -jnp.inf, F32)
        s_ref[hd, :, lo:] = st
        smax_ref[hd, :, lo:] = jnp.max(st, axis=0, keepdims=True)

    def step(j, j_next, next_diag=None, next_q_ref=q_ref):
        for hd in range(N_HEADS):
            st = s_ref[hd]
            m_old = m_ref[hd]
            m_new = jnp.maximum(m_old, smax_ref[hd])
            scores(j_next, hd, next_diag, next_q_ref)
            alpha = jnp.exp2(m_old - m_new)
            pt = jnp.exp2(st - m_new).astype(BF16)
            pv = jnp.dot(v_ref[0, hd, j], pt, preferred_element_type=F32)
            acc_ref[hd] = acc_ref[hd] * alpha + pv
            m_ref[hd] = m_new

    @pl.when(qi == 0)
    def _():
        for hd in range(N_HEADS):
            scores(n_off, hd, 0)

    for d in range(1, n_diag):
        step(n_off + d - 1, n_off + d, d)

    def group(i):
        j0 = i * n_diag
        step(jnp.where(i == 0, n_off + n_diag - 1, j0 - 1), j0)
        for d in range(1, n_diag):
            step(j0 + d - 1, j0 + d)

    def body(t, carry):
        for g in range(LOOP_GROUPS):
            group(t * LOOP_GROUPS + g)
        return carry

    n_body = qi // LOOP_GROUPS
    lax.fori_loop(0, n_body, body, 0)
    for g in range(LOOP_GROUPS - 1):
        @pl.when(n_body * LOOP_GROUPS + g < qi)
        def _():
            group(n_body * LOOP_GROUPS + g)

    last = jnp.where(n_off == 0, n_diag - 1, n_off - 1)
    step(last, jnp.minimum(n_off + n_diag, nblk - n_diag), 0, qnext_ref)

    for hd in range(N_HEADS):
        acc = acc_ref[hd]
        o_ref[0, hd * HEAD_DIM:(hd + 1) * HEAD_DIM, :] = (
            acc[:HEAD_DIM] / acc[HEAD_DIM:HEAD_DIM + 1]).astype(BF16)


def _attn(q, kx, v, *, batch, seq):
    nblk = seq // ATTN_BLOCK
    n_tiles = seq // Q_TILE
    q_spec = lambda tile: pl.BlockSpec((1, N_HEADS, QK_DEPTH, Q_TILE),
                                       lambda b, i: (b, 0, 0, tile(i)))
    return pl.pallas_call(
        functools.partial(_attn_kernel, nblk=nblk),
        grid=(batch, n_tiles),
        in_specs=[q_spec(lambda i: i), q_spec(lambda i: jnp.minimum(i + 1, n_tiles - 1)),
                  pl.BlockSpec((1, N_HEADS, seq, LANES), lambda b, i: (b, 0, 0, 0)),
                  pl.BlockSpec((1, N_HEADS, nblk, V_ROWS, ATTN_BLOCK), lambda b, i: (b, 0, 0, 0, 0))],
        out_specs=pl.BlockSpec((1, ATTN_WIDTH, Q_TILE), lambda b, i: (b, 0, i)),
        out_shape=jax.ShapeDtypeStruct((batch, ATTN_WIDTH, seq), BF16),
        scratch_shapes=[pltpu.VMEM((N_HEADS, ATTN_BLOCK, Q_TILE), F32),
                        pltpu.VMEM((N_HEADS, 1, Q_TILE), F32),
                        pltpu.VMEM((N_HEADS, 1, Q_TILE), F32),
                        pltpu.VMEM((N_HEADS, V_ROWS, Q_TILE), F32)],
        compiler_params=pltpu.CompilerParams(dimension_semantics=("arbitrary", "arbitrary"),
                                             vmem_limit_bytes=VMEM_LIMIT),
        name="attn",
    )(q, q, kx, v)


def _merge_kernel(h_ref, at_ref, sgu_ref, gain_ref, wgatet_ref, bgate_ref, wua_ref, wug_ref,
                  wout_ref, o_ref):
    for s in range(MERGE_TILE // TOKEN_TILE):
        rows = slice(s * TOKEN_TILE, (s + 1) * TOKEN_TILE)
        h = h_ref[rows, :]
        nb = _rms_norm(h, gain_ref[...]).astype(BF16)
        logits = jnp.dot(nb, wgatet_ref[...], preferred_element_type=F32)
        gates = jax.nn.sigmoid(logits + bgate_ref[...])
        up_a = lax.dot_general(at_ref[0, :, rows], wua_ref[...], (((0,), (0,)), ((), ())),
                               preferred_element_type=F32)
        up_g = jnp.dot(sgu_ref[rows, :], wug_ref[...], preferred_element_type=F32)
        merged = gates[:, :D_MODEL] * up_a + gates[:, D_MODEL:] * up_g
        o_ref[rows, :] = h + jnp.dot(merged.astype(BF16), wout_ref[...],
                                     preferred_element_type=F32)


def _merge(h, at, sgu, gain, wgate, bgate, wua, wug, wout, *, seq):
    t = h.shape[0]
    tiles_per_seq = seq // MERGE_TILE
    tok = lambda i: (i, 0)
    return pl.pallas_call(
        _merge_kernel,
        grid=(t // MERGE_TILE,),
        in_specs=[pl.BlockSpec((MERGE_TILE, D_MODEL), tok),
                  pl.BlockSpec((1, ATTN_WIDTH, MERGE_TILE),
                               lambda i: (i // tiles_per_seq, 0, i % tiles_per_seq)),
                  pl.BlockSpec((MERGE_TILE, SGU_WIDTH), tok),
                  _resident((1, D_MODEL)), _resident(wgate.shape), _resident(bgate.shape),
                  _resident(wua.shape), _resident(wug.shape), _resident(wout.shape)],
        out_specs=pl.BlockSpec((MERGE_TILE, D_MODEL), tok),
        out_shape=jax.ShapeDtypeStruct((t, D_MODEL), F32),
        compiler_params=_params(),
        name="merge",
    )(h, at, sgu, gain, wgate, bgate, wua, wug, wout)


def kernel(x, ffn1_norm, ffn1_w_in, ffn1_w_out, mix_norm, w_in, b_forget, b_gate, sgu_ln_gain,
           sgu_ln_bias, sgu_w_spatial, sgu_b_spatial, w_up_attn, w_up_sgu, w_out, ffn2_norm,
           ffn2_w_in, ffn2_w_out, final_norm):
    batch, seq, _ = x.shape
    depth = ffn1_norm.shape[0]
    row = lambda v: v.reshape(1, -1)
    fgain = row(final_norm)
    h = x.reshape(batch * seq, D_MODEL)
    for l in range(depth):
        last = l == depth - 1
        h, (wi2, wo2) = _ffn(h, row(ffn1_norm[l]), ffn1_w_in[l], ffn1_w_out[l], fgain,
                             final_norm=False, tile=TOKEN_TILE,
                             cast=(ffn2_w_in[l], ffn2_w_out[l]))

        o_k, o_v, o_f = ATTN_WIDTH, 2 * ATTN_WIDTH, 3 * ATTN_WIDTH
        o_u = o_f + N_HEADS
        o_vg, o_g = o_u + SGU_WIDTH, o_u + 2 * SGU_WIDTH
        wt = w_in[l].T
        wqt = (wt[:o_k] * (HEAD_DIM ** -0.5)).astype(BF16)
        wvt = wt[o_v:o_f].astype(BF16)
        wft = jnp.repeat(wt[o_f:o_u], N_SPLIT, axis=0)
        wft = jnp.pad(wft, ((0, LANES - N_SPLIT * N_HEADS), (0, 0)))
        wnt = jnp.concatenate([wt[o_k:o_v], wt[o_u:o_g], wft], axis=0).astype(BF16)
        bf = jnp.pad(jnp.repeat(b_forget[l], N_SPLIT), (0, LANES - N_SPLIT * N_HEADS)).reshape(1, LANES)
        wsp = sgu_w_spatial[l].reshape(N_GROUPS // SGU_PACK, SGU_PACK, CHUNK, CHUNK)
        wsp = wsp.transpose(0, 2, 1, 3).reshape(N_GROUPS // SGU_PACK, CHUNK, SGU_PACK * CHUNK)
        bsp = jnp.repeat(sgu_b_spatial[l].T, GROUP_DIM, axis=1)

        q, kx, v, sgu = _proj(h, row(mix_norm[l]), wqt, wvt, wnt, bf, row(sgu_ln_gain[l]),
                              row(sgu_ln_bias[l]), wsp, bsp, batch=batch, seq=seq)
        at = _attn(q, kx, v, batch=batch, seq=seq)
        h = _merge(h, at, sgu, row(mix_norm[l]), w_in[l][:, o_g:].astype(BF16), row(b_gate[l]),
                   w_up_attn[l].astype(BF16), w_up_sgu[l].astype(BF16), w_out[l].astype(BF16),
                   seq=seq)
        h, _ = _ffn(h, row(ffn2_norm[l]), wi2, wo2, fgain, final_norm=last, tile=FFN2_TILE)
    return h.reshape(batch, seq, D_MODEL)
```

```python
import functools

import jax
import jax.numpy as jnp
from jax import lax
from jax.experimental import pallas as pl
from jax.experimental.pallas import tpu as pltpu

F32 = jnp.float32
BF16 = jnp.bfloat16

D_MODEL = 1024
D_FF = 2816
N_HEADS = 8
HEAD_DIM = 64
ATTN_WIDTH = N_HEADS * HEAD_DIM
N_GROUPS = 8
GROUP_DIM = 64
SGU_WIDTH = N_GROUPS * GROUP_DIM
CHUNK = 128
RMS_EPS = 1e-6
LN_EPS = 1e-5
LOG2E = 1.4426950408889634

LANES = 128
BF16_SUBLANES = 16
V7X_VMEM_BYTES = 64 * 1024 * 1024
VMEM_LIMIT = V7X_VMEM_BYTES * 7 // 8

TOKEN_TILE = 512
PROJ_TILE = 1024
MERGE_TILE = 1024
FFN2_TILE = 1024
FFN_COLS = 256
ATTN_BLOCK = 256
Q_TILE = 512
LOOP_GROUPS = 2
V_ROWS = HEAD_DIM + 16
N_SPLIT = 3
QK_DEPTH = LANES
SGU_PACK = 4


def _rms_norm(x, gain):
    return (x * lax.rsqrt(jnp.mean(x * x, axis=-1, keepdims=True) + RMS_EPS)) * gain


def _gelu(x):
    return x * (lax.erf(x * (2.0 ** -0.5)) + 1.0) * 0.5


def _resident(shape):
    zeros = (0,) * len(shape)
    return pl.BlockSpec(shape, lambda *_: zeros, pipeline_mode=pl.Buffered(1))


def _params():
    return pltpu.CompilerParams(dimension_semantics=("arbitrary",),
                                vmem_limit_bytes=VMEM_LIMIT)


def _ffn_kernel(*refs, final_norm, tile, n_cast):
    x_ref, gain_ref, wi_ref, wo_ref, fgain_ref = refs[:5]
    cast_src = refs[5:5 + n_cast]
    o_ref = refs[5 + n_cast]
    cast_dst = refs[6 + n_cast:6 + 2 * n_cast]
    act_ref = refs[6 + 2 * n_cast]

    for src, dst in zip(cast_src, cast_dst):
        dst[...] = src[...].astype(dst.dtype)

    wdt = wi_ref.dtype
    for s in range(tile // TOKEN_TILE):
        rows = slice(s * TOKEN_TILE, (s + 1) * TOKEN_TILE)
        x = x_ref[rows, :]
        n = _rms_norm(x, gain_ref[...]).astype(wdt)
        for j in range(D_FF // FFN_COLS):
            lo = j * FFN_COLS
            gate = jnp.dot(n, wi_ref[:, lo:lo + FFN_COLS], preferred_element_type=F32)
            up = jnp.dot(n, wi_ref[:, D_FF + lo:D_FF + lo + FFN_COLS], preferred_element_type=F32)
            act_ref[s, :, lo:lo + FFN_COLS] = (gate * jax.nn.sigmoid(gate) * up).astype(wdt)
        y = jnp.dot(act_ref[s], wo_ref[...], preferred_element_type=F32)
        h = x + 0.5 * y
        if final_norm:
            h = _rms_norm(h, fgain_ref[...])
        o_ref[rows, :] = h


def _cast_slab_rows(n_rows, n_steps):
    for rows in range(BF16_SUBLANES, n_rows + 1, BF16_SUBLANES):
        if n_rows % rows == 0 and n_rows // rows <= n_steps:
            return rows
    return pl.cdiv(pl.cdiv(n_rows, n_steps), BF16_SUBLANES) * BF16_SUBLANES


def _ffn(x, gain, wi, wo, fgain, *, final_norm, tile, cast=()):
    t = x.shape[0]
    steps = t // tile
    tok = pl.BlockSpec((tile, D_MODEL), lambda i: (i, 0))
    slab_specs = []
    for arr in cast:
        rows = _cast_slab_rows(arr.shape[0], steps)
        last = pl.cdiv(arr.shape[0], rows) - 1
        slab_specs.append(pl.BlockSpec((rows, arr.shape[1]),
                                       lambda i, last=last: (jnp.minimum(i, last), 0)))
    outs = pl.pallas_call(
        functools.partial(_ffn_kernel, final_norm=final_norm, tile=tile, n_cast=len(cast)),
        grid=(steps,),
        in_specs=[tok, _resident((1, D_MODEL)), _resident(wi.shape), _resident(wo.shape),
                  _resident((1, D_MODEL))] + slab_specs,
        out_specs=[tok] + slab_specs,
        out_shape=[jax.ShapeDtypeStruct((t, D_MODEL), F32)]
                  + [jax.ShapeDtypeStruct(arr.shape, BF16) for arr in cast],
        scratch_shapes=[pltpu.VMEM((tile // TOKEN_TILE, TOKEN_TILE, D_FF), wi.dtype)],
        compiler_params=_params(),
        name="ffn_final" if final_norm else "ffn",
    )(x, gain, wi, wo, fgain, *cast)
    return outs[0], tuple(outs[1:])


def _split_bf16(x):
    parts = []
    for _ in range(N_SPLIT):
        p = x.astype(BF16).astype(F32)
        parts.append(p)
        x = x - p
    return parts


def _proj_kernel(h_ref, gain_ref, wqt_ref, wvt_ref, wnt_ref, bf_ref, lng_ref, lnb_ref,
                 wsp_ref, bsp_ref,
                 q_ref, kx_ref, v_ref, sgu_ref, carry_ref, *, tiles_per_seq):
    i = pl.program_id(0)

    @pl.when(i % tiles_per_seq == 0)
    def _():
        carry_ref[...] = jnp.zeros_like(carry_ref)

    tdims = (((1,), (1,)), ((), ()))
    sub = ATTN_BLOCK
    sel_row = lax.broadcasted_iota(jnp.int32, (QK_DEPTH - HEAD_DIM, sub), 0)
    ones_rows = jnp.where(sel_row < N_SPLIT, 1.0, 0.0).astype(BF16)
    pad_row = lax.broadcasted_iota(jnp.int32, (V_ROWS - HEAD_DIM, sub), 0)
    ones_row = jnp.where(pad_row == 0, 1.0, 0.0).astype(BF16)
    lane = lax.broadcasted_iota(jnp.int32, (CHUNK, LANES), 1)
    term = lane % N_SPLIT
    live = lane < N_SPLIT * N_HEADS

    def by_term(parts):
        out = parts[-1]
        for j in range(N_SPLIT - 2, -1, -1):
            out = jnp.where(term == j, parts[j], out)
        return out

    row = lax.broadcasted_iota(jnp.int32, (CHUNK, CHUNK), 0)
    col = lax.broadcasted_iota(jnp.int32, (CHUNK, CHUNK), 1)
    tri = jnp.where(row >= col, 1.0, 0.0).astype(BF16)
    pack_w = SGU_PACK * GROUP_DIM
    wrow = lax.broadcasted_iota(jnp.int32, (CHUNK, SGU_PACK * CHUNK), 0)
    wcol = lax.broadcasted_iota(jnp.int32, (CHUNK, SGU_PACK * CHUNK), 1) % CHUNK
    wsp = [jnp.where(wcol <= wrow, wsp_ref[p], 0.0).astype(BF16)
           for p in range(N_GROUPS // SGU_PACK)]
    lane_group = lax.broadcasted_iota(jnp.int32, (CHUNK, pack_w), 1) // GROUP_DIM

    def project(s):
        r0 = s * sub
        nb = _rms_norm(h_ref[r0:r0 + sub, :], gain_ref[...]).astype(BF16)
        qt = lax.dot_general(wqt_ref[...], nb, tdims, preferred_element_type=F32)
        qt = (qt * LOG2E).astype(BF16)
        vt = lax.dot_general(wvt_ref[...], nb, tdims, preferred_element_type=F32).astype(BF16)
        for hd in range(N_HEADS):
            q_ref[0, hd, :HEAD_DIM, r0:r0 + sub] = qt[hd * HEAD_DIM:(hd + 1) * HEAD_DIM, :]
            q_ref[0, hd, HEAD_DIM:, r0:r0 + sub] = ones_rows
            v_ref[0, hd, s, :HEAD_DIM, :] = vt[hd * HEAD_DIM:(hd + 1) * HEAD_DIM, :]
            v_ref[0, hd, s, HEAD_DIM:, :] = ones_row
        return lax.dot_general(nb, wnt_ref[...], tdims, preferred_element_type=F32)

    def tail(s, z, carry):
        r0 = s * sub
        zk = z[:, :ATTN_WIDTH]
        u = z[:, ATTN_WIDTH:ATTN_WIDTH + SGU_WIDTH]
        vg = z[:, ATTN_WIDTH + SGU_WIDTH:ATTN_WIDTH + 2 * SGU_WIDTH]
        f = z[:, ATTN_WIDTH + 2 * SGU_WIDTH:] + bf_ref[...]
        logf = jnp.minimum(f, 0.0) - jnp.log1p(jnp.exp(-jnp.abs(f)))

        gu = _gelu(u)
        gv = _gelu(vg)
        mu = jnp.mean(gv, axis=-1, keepdims=True)
        dv = gv - mu
        var = jnp.mean(dv * dv, axis=-1, keepdims=True)
        vn = ((dv * lax.rsqrt(var + LN_EPS)) * lng_ref[...] + lnb_ref[...]).astype(BF16)

        for c in range(sub // CHUNK):
            rows = slice(c * CHUNK, (c + 1) * CHUNK)
            out_rows = slice(r0 + c * CHUNK, r0 + (c + 1) * CHUNK)
            local = jnp.dot(tri, by_term(_split_bf16(logf[rows])).astype(BF16),
                            preferred_element_type=F32)
            shifted = {0: local}
            for d in range(1, N_SPLIT):
                shifted[d] = pltpu.roll(local, d, 1)
                shifted[-d] = pltpu.roll(local, LANES - d, 1)
            csum = carry
            for j in range(N_SPLIT):
                csum = csum + by_term([shifted[t - j] for t in range(N_SPLIT)])
            carry = csum[CHUNK - 1:, :]
            bias = jnp.where(live, by_term(_split_bf16(csum * -LOG2E)), 0.0)
            for hd in range(N_HEADS):
                pair = hd // 2
                kslab = zk[rows, pair * LANES:(pair + 1) * LANES]
                if hd % 2:
                    kslab = pltpu.roll(kslab, HEAD_DIM, 1)
                hbias = pltpu.roll(bias, HEAD_DIM - N_SPLIT * hd, 1)
                kx = jnp.where(lane < HEAD_DIM, kslab,
                               jnp.where(lane < HEAD_DIM + N_SPLIT, hbias, 0.0))
                kx_ref[0, hd, out_rows, :] = kx.astype(BF16)

            mixed = []
            for p in range(N_GROUPS // SGU_PACK):
                slab = vn[rows, p * pack_w:(p + 1) * pack_w]
                blocks = jnp.concatenate(
                    [jnp.where(lane_group == r, slab, jnp.zeros_like(slab))
                     for r in range(SGU_PACK)], axis=0)
                mixed.append(jnp.dot(wsp[p], blocks, preferred_element_type=F32))
            mixed = jnp.concatenate(mixed, axis=1) + bsp_ref[...]
            sgu_ref[out_rows, :] = (gu[rows] * mixed).astype(BF16)
        return carry

    n_sub = PROJ_TILE // sub
    carry = carry_ref[...]
    z_prev = project(0)
    for s in range(1, n_sub):
        z = project(s)
        carry = tail(s - 1, z_prev, carry)
        z_prev = z
    carry_ref[...] = tail(n_sub - 1, z_prev, carry)


def _proj(h, gain, wqt, wvt, wnt, bf, lng, lnb, wsp, bsp, *, batch, seq):
    t = batch * seq
    tiles_per_seq = seq // PROJ_TILE
    sub = PROJ_TILE // ATTN_BLOCK
    tok = lambda i: (i, 0)
    return pl.pallas_call(
        functools.partial(_proj_kernel, tiles_per_seq=tiles_per_seq),
        grid=(t // PROJ_TILE,),
        in_specs=[pl.BlockSpec((PROJ_TILE, D_MODEL), tok), _resident((1, D_MODEL)),
                  _resident(wqt.shape), _resident(wvt.shape), _resident(wnt.shape),
                  _resident((1, LANES)), _resident((1, SGU_WIDTH)), _resident((1, SGU_WIDTH)),
                  _resident(wsp.shape), _resident(bsp.shape)],
        out_specs=[pl.BlockSpec((1, N_HEADS, QK_DEPTH, PROJ_TILE),
                                lambda i: (i // tiles_per_seq, 0, 0, i % tiles_per_seq)),
                   pl.BlockSpec((1, N_HEADS, PROJ_TILE, LANES),
                                lambda i: (i // tiles_per_seq, 0, i % tiles_per_seq, 0)),
                   pl.BlockSpec((1, N_HEADS, sub, V_ROWS, ATTN_BLOCK),
                                lambda i: (i // tiles_per_seq, 0, i % tiles_per_seq, 0, 0)),
                   pl.BlockSpec((PROJ_TILE, SGU_WIDTH), tok)],
        out_shape=[jax.ShapeDtypeStruct((batch, N_HEADS, QK_DEPTH, seq), BF16),
                   jax.ShapeDtypeStruct((batch, N_HEADS, seq, LANES), BF16),
                   jax.ShapeDtypeStruct((batch, N_HEADS, seq // ATTN_BLOCK, V_ROWS, ATTN_BLOCK), BF16),
                   jax.ShapeDtypeStruct((t, SGU_WIDTH), BF16)],
        scratch_shapes=[pltpu.VMEM((1, LANES), F32)],
        compiler_params=_params(),
        name="proj",
    )(h, gain, wqt, wvt, wnt, bf, lng, lnb, wsp, bsp)


def _attn_kernel(q_ref, qnext_ref, kx_ref, v_ref, o_ref, s_ref, smax_ref, m_ref, acc_ref, *, nblk):
    blk = ATTN_BLOCK
    n_diag = Q_TILE // blk
    qi = pl.program_id(1)
    n_off = qi * n_diag

    m_ref[...] = jnp.full(m_ref.shape, -jnp.inf, F32)
    acc_ref[...] = jnp.zeros(acc_ref.shape, F32)

    key_pos = lax.broadcasted_iota(jnp.int32, (blk, Q_TILE), 0)
    qry_pos = lax.broadcasted_iota(jnp.int32, (blk, Q_TILE), 1)
    causal = key_pos <= qry_pos

    def scores(j, hd, diag, qt_ref=q_ref):
        rows = pl.ds(pl.multiple_of(j * blk, blk), blk)
        lo = 0 if diag is None else diag * blk
        st = jnp.dot(kx_ref[0, hd, rows, :], qt_ref[0, hd, :, lo:], preferred_element_type=F32)
        if diag is not None:
            st = jnp.where(causal[:, :Q_TILE - lo], st, -jnp.inf)
        if lo:
            s_ref[hd, :, :lo] = jnp.full((blk, lo), -jnp.inf, F32)
            smax_ref[hd, :, :lo] = jnp.full((1, lo), -jnp.inf, F32)
        s_ref[hd, :, lo:] = st
        smax_ref[hd, :, lo:] = jnp.max(st, axis=0, keepdims=True)

    def step(j, j_next, next_diag=None, next_q_ref=q_ref):
        for hd in range(N_HEADS):
            st = s_ref[hd]
            m_old = m_ref[hd]
            m_new = jnp.maximum(m_old, smax_ref[hd])
            scores(j_next, hd, next_diag, next_q_ref)
            alpha = jnp.exp2(m_old - m_new)
            pt = jnp.exp2(st - m_new).astype(BF16)
            pv = jnp.dot(v_ref[0, hd, j], pt, preferred_element_type=F32)
            acc_ref[hd] = acc_ref[hd] * alpha + pv
            m_ref[hd] = m_new

    @pl.when(qi == 0)
    def _():
        for hd in range(N_HEADS):
            scores(n_off, hd, 0)

    for d in range(1, n_diag):
        step(n_off + d - 1, n_off + d, d)

    def group(i):
        j0 = i * n_diag
        step(jnp.where(i == 0, n_off + n_diag - 1, j0 - 1), j0)
        for d in range(1, n_diag):
            step(j0 + d - 1, j0 + d)

    def body(t, carry):
        for g in range(LOOP_GROUPS):
            group(t * LOOP_GROUPS + g)
        return carry

    n_body = qi // LOOP_GROUPS
    lax.fori_loop(0, n_body, body, 0)
    for g in range(LOOP_GROUPS - 1):
        @pl.when(n_body * LOOP_GROUPS + g < qi)
        def _():
            group(n_body * LOOP_GROUPS + g)

    last = jnp.where(n_off == 0, n_diag - 1, n_off - 1)
    step(last, jnp.minimum(n_off + n_diag, nblk - n_diag), 0, qnext_ref)

    for hd in range(N_HEADS):
        acc = acc_ref[hd]
        o_ref[0, hd * HEAD_DIM:(hd + 1) * HEAD_DIM, :] = (
            acc[:HEAD_DIM] / acc[HEAD_DIM:HEAD_DIM + 1]).astype(BF16)


def _attn(q, kx, v, *, batch, seq):
    nblk = seq // ATTN_BLOCK
    n_tiles = seq // Q_TILE
    q_spec = lambda tile: pl.BlockSpec((1, N_HEADS, QK_DEPTH, Q_TILE),
                                       lambda b, i: (b, 0, 0, tile(i)))
    return pl.pallas_call(
        functools.partial(_attn_kernel, nblk=nblk),
        grid=(batch, n_tiles),
        in_specs=[q_spec(lambda i: i), q_spec(lambda i: jnp.minimum(i + 1, n_tiles - 1)),
                  pl.BlockSpec((1, N_HEADS, seq, LANES), lambda b, i: (b, 0, 0, 0)),
                  pl.BlockSpec((1, N_HEADS, nblk, V_ROWS, ATTN_BLOCK), lambda b, i: (b, 0, 0, 0, 0))],
        out_specs=pl.BlockSpec((1, ATTN_WIDTH, Q_TILE), lambda b, i: (b, 0, i)),
        out_shape=jax.ShapeDtypeStruct((batch, ATTN_WIDTH, seq), BF16),
        scratch_shapes=[pltpu.VMEM((N_HEADS, ATTN_BLOCK, Q_TILE), F32),
                        pltpu.VMEM((N_HEADS, 1, Q_TILE), F32),
                        pltpu.VMEM((N_HEADS, 1, Q_TILE), F32),
                        pltpu.VMEM((N_HEADS, V_ROWS, Q_TILE), F32)],
        compiler_params=pltpu.CompilerParams(dimension_semantics=("arbitrary", "arbitrary"),
                                             vmem_limit_bytes=VMEM_LIMIT),
        name="attn",
    )(q, q, kx, v)


def _merge_kernel(h_ref, at_ref, sgu_ref, gain_ref, wgatet_ref, bgate_ref, wua_ref, wug_ref,
                  wout_ref, o_ref):
    for s in range(MERGE_TILE // TOKEN_TILE):
        rows = slice(s * TOKEN_TILE, (s + 1) * TOKEN_TILE)
        h = h_ref[rows, :]
        nb = _rms_norm(h, gain_ref[...]).astype(BF16)
        logits = lax.dot_general(nb, wgatet_ref[...], (((1,), (1,)), ((), ())),
                                 preferred_element_type=F32)
        gates = jax.nn.sigmoid(logits + bgate_ref[...])
        up_a = lax.dot_general(at_ref[0, :, rows], wua_ref[...], (((0,), (0,)), ((), ())),
                               preferred_element_type=F32)
        up_g = jnp.dot(sgu_ref[rows, :], wug_ref[...], preferred_element_type=F32)
        merged = gates[:, :D_MODEL] * up_a + gates[:, D_MODEL:] * up_g
        o_ref[rows, :] = h + jnp.dot(merged.astype(BF16), wout_ref[...],
                                     preferred_element_type=F32)


def _merge(h, at, sgu, gain, wgate, bgate, wua, wug, wout, *, seq):
    t = h.shape[0]
    tiles_per_seq = seq // MERGE_TILE
    tok = lambda i: (i, 0)
    return pl.pallas_call(
        _merge_kernel,
        grid=(t // MERGE_TILE,),
        in_specs=[pl.BlockSpec((MERGE_TILE, D_MODEL), tok),
                  pl.BlockSpec((1, ATTN_WIDTH, MERGE_TILE),
                               lambda i: (i // tiles_per_seq, 0, i % tiles_per_seq)),
                  pl.BlockSpec((MERGE_TILE, SGU_WIDTH), tok),
                  _resident((1, D_MODEL)), _resident(wgate.shape), _resident(bgate.shape),
                  _resident(wua.shape), _resident(wug.shape), _resident(wout.shape)],
        out_specs=pl.BlockSpec((MERGE_TILE, D_MODEL), tok),
        out_shape=jax.ShapeDtypeStruct((t, D_MODEL), F32),
        compiler_params=_params(),
        name="merge",
    )(h, at, sgu, gain, wgate, bgate, wua, wug, wout)


def kernel(x, ffn1_norm, ffn1_w_in, ffn1_w_out, mix_norm, w_in, b_forget, b_gate, sgu_ln_gain,
           sgu_ln_bias, sgu_w_spatial, sgu_b_spatial, w_up_attn, w_up_sgu, w_out, ffn2_norm,
           ffn2_w_in, ffn2_w_out, final_norm):
    batch, seq, _ = x.shape
    depth = ffn1_norm.shape[0]
    row = lambda v: v.reshape(1, -1)
    fgain = row(final_norm)
    h = x.reshape(batch * seq, D_MODEL)
    for l in range(depth):
        last = l == depth - 1
        h, (wi2, wo2, wtb, wua, wug, wout) = _ffn(
            h, row(ffn1_norm[l]), ffn1_w_in[l], ffn1_w_out[l], fgain, final_norm=False,
            tile=TOKEN_TILE, cast=(ffn2_w_in[l], ffn2_w_out[l], w_in[l].T, w_up_attn[l],
                                   w_up_sgu[l], w_out[l]))

        o_k, o_v, o_f = ATTN_WIDTH, 2 * ATTN_WIDTH, 3 * ATTN_WIDTH
        o_u = o_f + N_HEADS
        o_vg, o_g = o_u + SGU_WIDTH, o_u + 2 * SGU_WIDTH
        wt = wtb
        wqt = wt[:o_k] * jnp.asarray(HEAD_DIM ** -0.5, BF16)
        wvt = wt[o_v:o_f]
        wft = jnp.repeat(wt[o_f:o_u], N_SPLIT, axis=0)
        wft = jnp.pad(wft, ((0, LANES - N_SPLIT * N_HEADS), (0, 0)))
        wnt = jnp.concatenate([wt[o_k:o_v], wt[o_u:o_g], wft], axis=0)
        bf = jnp.pad(jnp.repeat(b_forget[l], N_SPLIT), (0, LANES - N_SPLIT * N_HEADS)).reshape(1, LANES)
        wsp = sgu_w_spatial[l].reshape(N_GROUPS // SGU_PACK, SGU_PACK, CHUNK, CHUNK)
        wsp = wsp.transpose(0, 2, 1, 3).reshape(N_GROUPS // SGU_PACK, CHUNK, SGU_PACK * CHUNK)
        bsp = jnp.repeat(sgu_b_spatial[l].T, GROUP_DIM, axis=1)

        q, kx, v, sgu = _proj(h, row(mix_norm[l]), wqt, wvt, wnt, bf, row(sgu_ln_gain[l]),
                              row(sgu_ln_bias[l]), wsp, bsp, batch=batch, seq=seq)
        at = _attn(q, kx, v, batch=batch, seq=seq)
        h = _merge(h, at, sgu, row(mix_norm[l]), wt[o_g:], row(b_gate[l]), wua, wug, wout,
                   seq=seq)
        h, _ = _ffn(h, row(ffn2_norm[l]), wi2, wo2, fgain, final_norm=last, tile=FFN2_TILE)
    return h.reshape(batch, seq, D_MODEL)
```

```python
import functools

import jax
import jax.numpy as jnp
from jax import lax
from jax.experimental import pallas as pl
from jax.experimental.pallas import tpu as pltpu

F32 = jnp.float32
BF16 = jnp.bfloat16

D_MODEL = 1024
D_FF = 2816
N_HEADS = 8
HEAD_DIM = 64
ATTN_WIDTH = N_HEADS * HEAD_DIM
N_GROUPS = 8
GROUP_DIM = 64
SGU_WIDTH = N_GROUPS * GROUP_DIM
CHUNK = 128
RMS_EPS = 1e-6
LN_EPS = 1e-5
LOG2E = 1.4426950408889634

LANES = 128
BF16_SUBLANES = 16
V7X_VMEM_BYTES = 64 * 1024 * 1024
VMEM_LIMIT = V7X_VMEM_BYTES * 7 // 8

TOKEN_TILE = 512
PROJ_TILE = 1024
MERGE_TILE = 1024
FFN2_TILE = 1024
FFN_COLS = 256
ATTN_BLOCK = 256
Q_TILE = 512
LOOP_GROUPS = 2
V_ROWS = HEAD_DIM + 16
N_SPLIT = 3
QK_DEPTH = LANES
SGU_PACK = 4


def _rms_norm(x, gain):
    return (x * lax.rsqrt(jnp.mean(x * x, axis=-1, keepdims=True) + RMS_EPS)) * gain


def _gelu(x):
    return x * (lax.erf(x * (2.0 ** -0.5)) + 1.0) * 0.5


def _resident(shape):
    zeros = (0,) * len(shape)
    return pl.BlockSpec(shape, lambda *_: zeros, pipeline_mode=pl.Buffered(1))


def _params():
    return pltpu.CompilerParams(dimension_semantics=("arbitrary",),
                                vmem_limit_bytes=VMEM_LIMIT)


def _ffn_kernel(*refs, final_norm, tile, n_cast):
    x_ref, gain_ref, wi_ref, wo_ref, fgain_ref = refs[:5]
    cast_src = refs[5:5 + n_cast]
    o_ref = refs[5 + n_cast]
    cast_dst = refs[6 + n_cast:6 + 2 * n_cast]
    act_ref = refs[6 + 2 * n_cast]

    for src, dst in zip(cast_src, cast_dst):
        dst[...] = src[...].astype(dst.dtype)

    wdt = wi_ref.dtype
    for s in range(tile // TOKEN_TILE):
        rows = slice(s * TOKEN_TILE, (s + 1) * TOKEN_TILE)
        x = x_ref[rows, :]
        n = _rms_norm(x, gain_ref[...]).astype(wdt)
        for j in range(D_FF // FFN_COLS):
            lo = j * FFN_COLS
            gate = jnp.dot(n, wi_ref[:, lo:lo + FFN_COLS], preferred_element_type=F32)
            up = jnp.dot(n, wi_ref[:, D_FF + lo:D_FF + lo + FFN_COLS], preferred_element_type=F32)
            act_ref[s, :, lo:lo + FFN_COLS] = (gate * jax.nn.sigmoid(gate) * up).astype(wdt)
        y = jnp.dot(act_ref[s], wo_ref[...], preferred_element_type=F32)
        h = x + 0.5 * y
        if final_norm:
            h = _rms_norm(h, fgain_ref[...])
        o_ref[rows, :] = h


def _cast_slab_rows(n_rows, n_steps):
    for rows in range(BF16_SUBLANES, n_rows + 1, BF16_SUBLANES):
        if n_rows % rows == 0 and n_rows // rows <= n_steps:
            return rows
    raise ValueError((n_rows, n_steps))


def _ffn(x, gain, wi, wo, fgain, *, final_norm, tile, cast=()):
    t = x.shape[0]
    steps = t // tile
    tok = pl.BlockSpec((tile, D_MODEL), lambda i: (i, 0))
    slab_specs = []
    for arr in cast:
        rows = _cast_slab_rows(arr.shape[0], steps)
        last = arr.shape[0] // rows - 1
        slab_specs.append(pl.BlockSpec((rows, arr.shape[1]),
                                       lambda i, last=last: (jnp.minimum(i, last), 0)))
    outs = pl.pallas_call(
        functools.partial(_ffn_kernel, final_norm=final_norm, tile=tile, n_cast=len(cast)),
        grid=(steps,),
        in_specs=[tok, _resident((1, D_MODEL)), _resident(wi.shape), _resident(wo.shape),
                  _resident((1, D_MODEL))] + slab_specs,
        out_specs=[tok] + slab_specs,
        out_shape=[jax.ShapeDtypeStruct((t, D_MODEL), F32)]
                  + [jax.ShapeDtypeStruct(arr.shape, BF16) for arr in cast],
        scratch_shapes=[pltpu.VMEM((tile // TOKEN_TILE, TOKEN_TILE, D_FF), wi.dtype)],
        compiler_params=_params(),
        name="ffn_final" if final_norm else "ffn",
    )(x, gain, wi, wo, fgain, *cast)
    return outs[0], tuple(outs[1:])


def _split_bf16(x):
    parts = []
    for _ in range(N_SPLIT):
        p = x.astype(BF16).astype(F32)
        parts.append(p)
        x = x - p
    return parts


def _proj_kernel(h_ref, gain_ref, wqt_ref, wvt_ref, wnt_ref, bf_ref, lng_ref, lnb_ref,
                 wsp_ref, bsp_ref,
                 q_ref, kx_ref, v_ref, sgu_ref, carry_ref, *, tiles_per_seq):
    i = pl.program_id(0)

    @pl.when(i % tiles_per_seq == 0)
    def _():
        carry_ref[...] = jnp.zeros_like(carry_ref)

    tdims = (((1,), (1,)), ((), ()))
    sub = ATTN_BLOCK
    sel_row = lax.broadcasted_iota(jnp.int32, (QK_DEPTH - HEAD_DIM, sub), 0)
    ones_rows = jnp.where(sel_row < N_SPLIT, 1.0, 0.0).astype(BF16)
    pad_row = lax.broadcasted_iota(jnp.int32, (V_ROWS - HEAD_DIM, sub), 0)
    ones_row = jnp.where(pad_row == 0, 1.0, 0.0).astype(BF16)
    lane = lax.broadcasted_iota(jnp.int32, (CHUNK, LANES), 1)
    term = lane % N_SPLIT
    live = lane < N_SPLIT * N_HEADS

    def by_term(parts):
        out = parts[-1]
        for j in range(N_SPLIT - 2, -1, -1):
            out = jnp.where(term == j, parts[j], out)
        return out

    row = lax.broadcasted_iota(jnp.int32, (CHUNK, CHUNK), 0)
    col = lax.broadcasted_iota(jnp.int32, (CHUNK, CHUNK), 1)
    tri = jnp.where(row >= col, 1.0, 0.0).astype(BF16)
    pack_w = SGU_PACK * GROUP_DIM
    wrow = lax.broadcasted_iota(jnp.int32, (CHUNK, SGU_PACK * CHUNK), 0)
    wcol = lax.broadcasted_iota(jnp.int32, (CHUNK, SGU_PACK * CHUNK), 1) % CHUNK
    wsp = [jnp.where(wcol <= wrow, wsp_ref[p], 0.0).astype(BF16)
           for p in range(N_GROUPS // SGU_PACK)]
    lane_group = lax.broadcasted_iota(jnp.int32, (CHUNK, pack_w), 1) // GROUP_DIM

    def project(s):
        r0 = s * sub
        nb = _rms_norm(h_ref[r0:r0 + sub, :], gain_ref[...]).astype(BF16)
        qt = lax.dot_general(wqt_ref[...], nb, tdims, preferred_element_type=F32)
        qt = (qt * LOG2E).astype(BF16)
        vt = lax.dot_general(wvt_ref[...], nb, tdims, preferred_element_type=F32).astype(BF16)
        for hd in range(N_HEADS):
            q_ref[0, hd, :HEAD_DIM, r0:r0 + sub] = qt[hd * HEAD_DIM:(hd + 1) * HEAD_DIM, :]
            q_ref[0, hd, HEAD_DIM:, r0:r0 + sub] = ones_rows
            v_ref[0, hd, s, :HEAD_DIM, :] = vt[hd * HEAD_DIM:(hd + 1) * HEAD_DIM, :]
            v_ref[0, hd, s, HEAD_DIM:, :] = ones_row
        return lax.dot_general(nb, wnt_ref[...], tdims, preferred_element_type=F32)

    def tail(s, z, carry):
        r0 = s * sub
        zk = z[:, :ATTN_WIDTH]
        u = z[:, ATTN_WIDTH:ATTN_WIDTH + SGU_WIDTH]
        vg = z[:, ATTN_WIDTH + SGU_WIDTH:ATTN_WIDTH + 2 * SGU_WIDTH]
        f = z[:, ATTN_WIDTH + 2 * SGU_WIDTH:] + bf_ref[...]
        logf = jnp.minimum(f, 0.0) - jnp.log1p(jnp.exp(-jnp.abs(f)))

        gu = _gelu(u)
        gv = _gelu(vg)
        mu = jnp.mean(gv, axis=-1, keepdims=True)
        dv = gv - mu
        var = jnp.mean(dv * dv, axis=-1, keepdims=True)
        vn = ((dv * lax.rsqrt(var + LN_EPS)) * lng_ref[...] + lnb_ref[...]).astype(BF16)

        for c in range(sub // CHUNK):
            rows = slice(c * CHUNK, (c + 1) * CHUNK)
            out_rows = slice(r0 + c * CHUNK, r0 + (c + 1) * CHUNK)
            local = jnp.dot(tri, by_term(_split_bf16(logf[rows])).astype(BF16),
                            preferred_element_type=F32)
            shifted = {0: local}
            for d in range(1, N_SPLIT):
                shifted[d] = pltpu.roll(local, d, 1)
                shifted[-d] = pltpu.roll(local, LANES - d, 1)
            csum = carry
            for j in range(N_SPLIT):
                csum = csum + by_term([shifted[t - j] for t in range(N_SPLIT)])
            carry = csum[CHUNK - 1:, :]
            bias = jnp.where(live, by_term(_split_bf16(csum * -LOG2E)), 0.0)
            for hd in range(N_HEADS):
                pair = hd // 2
                kslab = zk[rows, pair * LANES:(pair + 1) * LANES]
                if hd % 2:
                    kslab = pltpu.roll(kslab, HEAD_DIM, 1)
                hbias = pltpu.roll(bias, HEAD_DIM - N_SPLIT * hd, 1)
                kx = jnp.where(lane < HEAD_DIM, kslab,
                               jnp.where(lane < HEAD_DIM + N_SPLIT, hbias, 0.0))
                kx_ref[0, hd, out_rows, :] = kx.astype(BF16)

            mixed = []
            for p in range(N_GROUPS // SGU_PACK):
                slab = vn[rows, p * pack_w:(p + 1) * pack_w]
                blocks = jnp.concatenate(
                    [jnp.where(lane_group == r, slab, jnp.zeros_like(slab))
                     for r in range(SGU_PACK)], axis=0)
                mixed.append(jnp.dot(wsp[p], blocks, preferred_element_type=F32))
            mixed = jnp.concatenate(mixed, axis=1) + bsp_ref[...]
            sgu_ref[out_rows, :] = (gu[rows] * mixed).astype(BF16)
        return carry

    n_sub = PROJ_TILE // sub
    carry = carry_ref[...]
    z_prev = project(0)
    for s in range(1, n_sub):
        z = project(s)
        carry = tail(s - 1, z_prev, carry)
        z_prev = z
    carry_ref[...] = tail(n_sub - 1, z_prev, carry)


def _proj(h, gain, wqt, wvt, wnt, bf, lng, lnb, wsp, bsp, *, batch, seq):
    t = batch * seq
    tiles_per_seq = seq // PROJ_TILE
    sub = PROJ_TILE // ATTN_BLOCK
    tok = lambda i: (i, 0)
    return pl.pallas_call(
        functools.partial(_proj_kernel, tiles_per_seq=tiles_per_seq),
        grid=(t // PROJ_TILE,),
        in_specs=[pl.BlockSpec((PROJ_TILE, D_MODEL), tok), _resident((1, D_MODEL)),
                  _resident(wqt.shape), _resident(wvt.shape), _resident(wnt.shape),
                  _resident((1, LANES)), _resident((1, SGU_WIDTH)), _resident((1, SGU_WIDTH)),
                  _resident(wsp.shape), _resident(bsp.shape)],
        out_specs=[pl.BlockSpec((1, N_HEADS, QK_DEPTH, PROJ_TILE),
                                lambda i: (i // tiles_per_seq, 0, 0, i % tiles_per_seq)),
                   pl.BlockSpec((1, N_HEADS, PROJ_TILE, LANES),
                                lambda i: (i // tiles_per_seq, 0, i % tiles_per_seq, 0)),
                   pl.BlockSpec((1, N_HEADS, sub, V_ROWS, ATTN_BLOCK),
                                lambda i: (i // tiles_per_seq, 0, i % tiles_per_seq, 0, 0)),
                   pl.BlockSpec((PROJ_TILE, SGU_WIDTH), tok)],
        out_shape=[jax.ShapeDtypeStruct((batch, N_HEADS, QK_DEPTH, seq), BF16),
                   jax.ShapeDtypeStruct((batch, N_HEADS, seq, LANES), BF16),
                   jax.ShapeDtypeStruct((batch, N_HEADS, seq // ATTN_BLOCK, V_ROWS, ATTN_BLOCK), BF16),
                   jax.ShapeDtypeStruct((t, SGU_WIDTH), BF16)],
        scratch_shapes=[pltpu.VMEM((1, LANES), F32)],
        compiler_params=_params(),
        name="proj",
    )(h, gain, wqt, wvt, wnt, bf, lng, lnb, wsp, bsp)


def _attn_kernel(q_ref, qnext_ref, kx_ref, v_ref, o_ref, s_ref, smax_ref, m_ref, acc_ref, *, nblk):
    blk = ATTN_BLOCK
    n_diag = Q_TILE // blk
    qi = pl.program_id(1)
    n_off = qi * n_diag

    m_ref[...] = jnp.full(m_ref.shape, -jnp.inf, F32)
    acc_ref[...] = jnp.zeros(acc_ref.shape, F32)

    key_pos = lax.broadcasted_iota(jnp.int32, (blk, Q_TILE), 0)
    qry_pos = lax.broadcasted_iota(jnp.int32, (blk, Q_TILE), 1)
    causal = key_pos <= qry_pos

    def scores(j, hd, diag, qt_ref=q_ref):
        rows = pl.ds(pl.multiple_of(j * blk, blk), blk)
        lo = 0 if diag is None else diag * blk
        st = jnp.dot(kx_ref[0, hd, rows, :], qt_ref[0, hd, :, lo:], preferred_element_type=F32)
        if diag is not None:
            st = jnp.where(causal[:, :Q_TILE - lo], st, -jnp.inf)
        if lo:
            s_ref[hd, :, :lo] = jnp.full((blk, lo), -jnp.inf, F32)
            smax_ref[hd, :, :lo] = jnp.full((1, lo), -jnp.inf, F32)
        s_ref[hd, :, lo:] = st
        smax_ref[hd, :, lo:] = jnp.max(st, axis=0, keepdims=True)

    def step(j, j_next, next_diag=None, next_q_ref=q_ref):
        for hd in range(N_HEADS):
            st = s_ref[hd]
            m_old = m_ref[hd]
            m_new = jnp.maximum(m_old, smax_ref[hd])
            scores(j_next, hd, next_diag, next_q_ref)
            alpha = jnp.exp2(m_old - m_new)
            pt = jnp.exp2(st - m_new).astype(BF16)
            pv = jnp.dot(v_ref[0, hd, j], pt, preferred_element_type=F32)
            acc_ref[hd] = acc_ref[hd] * alpha + pv
            m_ref[hd] = m_new

    @pl.when(qi == 0)
    def _():
        for hd in range(N_HEADS):
            scores(n_off, hd, 0)

    for d in range(1, n_diag):
        step(n_off + d - 1, n_off + d, d)

    def group(i):
        j0 = i * n_diag
        step(jnp.where(i == 0, n_off + n_diag - 1, j0 - 1), j0)
        for d in range(1, n_diag):
            step(j0 + d - 1, j0 + d)

    def body(t, carry):
        for g in range(LOOP_GROUPS):
            group(t * LOOP_GROUPS + g)
        return carry

    n_body = qi // LOOP_GROUPS
    lax.fori_loop(0, n_body, body, 0)
    for g in range(LOOP_GROUPS - 1):
        @pl.when(n_body * LOOP_GROUPS + g < qi)
        def _():
            group(n_body * LOOP_GROUPS + g)

    last = jnp.where(n_off == 0, n_diag - 1, n_off - 1)
    step(last, jnp.minimum(n_off + n_diag, nblk - n_diag), 0, qnext_ref)

    for hd in range(N_HEADS):
        acc = acc_ref[hd]
        o_ref[0, hd * HEAD_DIM:(hd + 1) * HEAD_DIM, :] = (
            acc[:HEAD_DIM] / acc[HEAD_DIM:HEAD_DIM + 1]).astype(BF16)


def _attn(q, kx, v, *, batch, seq):
    nblk = seq // ATTN_BLOCK
    n_tiles = seq // Q_TILE
    q_spec = lambda tile: pl.BlockSpec((1, N_HEADS, QK_DEPTH, Q_TILE),
                                       lambda b, i: (b, 0, 0, tile(i)))
    return pl.pallas_call(
        functools.partial(_attn_kernel, nblk=nblk),
        grid=(batch, n_tiles),
        in_specs=[q_spec(lambda i: i), q_spec(lambda i: jnp.minimum(i + 1, n_tiles - 1)),
                  pl.BlockSpec((1, N_HEADS, seq, LANES), lambda b, i: (b, 0, 0, 0)),
                  pl.BlockSpec((1, N_HEADS, nblk, V_ROWS, ATTN_BLOCK), lambda b, i: (b, 0, 0, 0, 0))],
        out_specs=pl.BlockSpec((1, ATTN_WIDTH, Q_TILE), lambda b, i: (b, 0, i)),
        out_shape=jax.ShapeDtypeStruct((batch, ATTN_WIDTH, seq), BF16),
        scratch_shapes=[pltpu.VMEM((N_HEADS, ATTN_BLOCK, Q_TILE), F32),
                        pltpu.VMEM((N_HEADS, 1, Q_TILE), F32),
                        pltpu.VMEM((N_HEADS, 1, Q_TILE), F32),
                        pltpu.VMEM((N_HEADS, V_ROWS, Q_TILE), F32)],
        compiler_params=pltpu.CompilerParams(dimension_semantics=("arbitrary", "arbitrary"),
                                             vmem_limit_bytes=VMEM_LIMIT),
        name="attn",
    )(q, q, kx, v)


def _merge_kernel(h_ref, at_ref, sgu_ref, gain_ref, wgatet_ref, bgate_ref, wua_ref, wug_ref,
                  wout_ref, o_ref):
    for s in range(MERGE_TILE // TOKEN_TILE):
        rows = slice(s * TOKEN_TILE, (s + 1) * TOKEN_TILE)
        h = h_ref[rows, :]
        nb = _rms_norm(h, gain_ref[...]).astype(BF16)
        logits = lax.dot_general(nb, wgatet_ref[...], (((1,), (1,)), ((), ())),
                                 preferred_element_type=F32)
        gates = jax.nn.sigmoid(logits + bgate_ref[...])
        up_a = lax.dot_general(at_ref[0, :, rows], wua_ref[...], (((0,), (0,)), ((), ())),
                               preferred_element_type=F32)
        up_g = jnp.dot(sgu_ref[rows, :], wug_ref[...], preferred_element_type=F32)
        merged = gates[:, :D_MODEL] * up_a + gates[:, D_MODEL:] * up_g
        o_ref[rows, :] = h + jnp.dot(merged.astype(BF16), wout_ref[...],
                                     preferred_element_type=F32)


def _merge(h, at, sgu, gain, wgate, bgate, wua, wug, wout, *, seq):
    t = h.shape[0]
    tiles_per_seq = seq // MERGE_TILE
    tok = lambda i: (i, 0)
    return pl.pallas_call(
        _merge_kernel,
        grid=(t // MERGE_TILE,),
        in_specs=[pl.BlockSpec((MERGE_TILE, D_MODEL), tok),
                  pl.BlockSpec((1, ATTN_WIDTH, MERGE_TILE),
                               lambda i: (i // tiles_per_seq, 0, i % tiles_per_seq)),
                  pl.BlockSpec((MERGE_TILE, SGU_WIDTH), tok),
                  _resident((1, D_MODEL)), _resident(wgate.shape), _resident(bgate.shape),
                  _resident(wua.shape), _resident(wug.shape), _resident(wout.shape)],
        out_specs=pl.BlockSpec((MERGE_TILE, D_MODEL), tok),
        out_shape=jax.ShapeDtypeStruct((t, D_MODEL), F32),
        compiler_params=_params(),
        name="merge",
    )(h, at, sgu, gain, wgate, bgate, wua, wug, wout)


def kernel(x, ffn1_norm, ffn1_w_in, ffn1_w_out, mix_norm, w_in, b_forget, b_gate, sgu_ln_gain,
           sgu_ln_bias, sgu_w_spatial, sgu_b_spatial, w_up_attn, w_up_sgu, w_out, ffn2_norm,
           ffn2_w_in, ffn2_w_out, final_norm):
    batch, seq, _ = x.shape
    depth = ffn1_norm.shape[0]
    row = lambda v: v.reshape(1, -1)
    fgain = row(final_norm)
    h = x.reshape(batch * seq, D_MODEL)
    for l in range(depth):
        last = l == depth - 1
        h, (wi2, wo2) = _ffn(h, row(ffn1_norm[l]), ffn1_w_in[l], ffn1_w_out[l], fgain,
                             final_norm=False, tile=TOKEN_TILE,
                             cast=(ffn2_w_in[l], ffn2_w_out[l]))

        o_k, o_v, o_f = ATTN_WIDTH, 2 * ATTN_WIDTH, 3 * ATTN_WIDTH
        o_u = o_f + N_HEADS
        o_vg, o_g = o_u + SGU_WIDTH, o_u + 2 * SGU_WIDTH
        wt = w_in[l].T
        wqt = (wt[:o_k] * (HEAD_DIM ** -0.5)).astype(BF16)
        wvt = wt[o_v:o_f].astype(BF16)
        wft = jnp.repeat(wt[o_f:o_u], N_SPLIT, axis=0)
        wft = jnp.pad(wft, ((0, LANES - N_SPLIT * N_HEADS), (0, 0)))
        wnt = jnp.concatenate([wt[o_k:o_v], wt[o_u:o_g], wft], axis=0).astype(BF16)
        bf = jnp.pad(jnp.repeat(b_forget[l], N_SPLIT), (0, LANES - N_SPLIT * N_HEADS)).reshape(1, LANES)
        wsp = sgu_w_spatial[l].reshape(N_GROUPS // SGU_PACK, SGU_PACK, CHUNK, CHUNK)
        wsp = wsp.transpose(0, 2, 1, 3).reshape(N_GROUPS // SGU_PACK, CHUNK, SGU_PACK * CHUNK)
        bsp = jnp.repeat(sgu_b_spatial[l].T, GROUP_DIM, axis=1)

        q, kx, v, sgu = _proj(h, row(mix_norm[l]), wqt, wvt, wnt, bf, row(sgu_ln_gain[l]),
                              row(sgu_ln_bias[l]), wsp, bsp, batch=batch, seq=seq)
        at = _attn(q, kx, v, batch=batch, seq=seq)
        h = _merge(h, at, sgu, row(mix_norm[l]), wt[o_g:].astype(BF16), row(b_gate[l]),
                   w_up_attn[l].astype(BF16), w_up_sgu[l].astype(BF16), w_out[l].astype(BF16),
                   seq=seq)
        h, _ = _ffn(h, row(ffn2_norm[l]), wi2, wo2, fgain, final_norm=last, tile=FFN2_TILE)
    return h.reshape(batch, seq, D_MODEL)
```

```python
import functools

import jax
import jax.numpy as jnp
from jax import lax
from jax.experimental import pallas as pl
from jax.experimental.pallas import tpu as pltpu

F32 = jnp.float32
BF16 = jnp.bfloat16

D_MODEL = 1024
D_FF = 2816
N_HEADS = 8
HEAD_DIM = 64
ATTN_WIDTH = N_HEADS * HEAD_DIM
N_GROUPS = 8
GROUP_DIM = 64
SGU_WIDTH = N_GROUPS * GROUP_DIM
CHUNK = 128
RMS_EPS = 1e-6
LN_EPS = 1e-5
LOG2E = 1.4426950408889634

LANES = 128
BF16_SUBLANES = 16
V7X_VMEM_BYTES = 64 * 1024 * 1024
VMEM_LIMIT = V7X_VMEM_BYTES * 7 // 8

TOKEN_TILE = 512
PROJ_TILE = 1024
MERGE_TILE = 1024
FFN2_TILE = 1024
FFN_COLS = 256
ATTN_BLOCK = 256
Q_TILE = 512
LOOP_GROUPS = 2
V_ROWS = HEAD_DIM + 16
N_SPLIT = 3
QK_DEPTH = LANES
SGU_PACK = 4


def _rms_norm(x, gain):
    return (x * lax.rsqrt(jnp.mean(x * x, axis=-1, keepdims=True) + RMS_EPS)) * gain


def _gelu(x):
    return x * (lax.erf(x * (2.0 ** -0.5)) + 1.0) * 0.5


def _resident(shape):
    zeros = (0,) * len(shape)
    return pl.BlockSpec(shape, lambda *_: zeros, pipeline_mode=pl.Buffered(1))


def _params():
    return pltpu.CompilerParams(dimension_semantics=("arbitrary",),
                                vmem_limit_bytes=VMEM_LIMIT)


def _ffn_kernel(*refs, final_norm, tile, n_cast):
    x_ref, gain_ref, wi_ref, wo_ref, fgain_ref = refs[:5]
    cast_src = refs[5:5 + n_cast]
    o_ref = refs[5 + n_cast]
    cast_dst = refs[6 + n_cast:6 + 2 * n_cast]
    act_ref = refs[6 + 2 * n_cast]

    for src, dst in zip(cast_src, cast_dst):
        dst[...] = src[...].astype(dst.dtype)

    wdt = wi_ref.dtype
    for s in range(tile // TOKEN_TILE):
        rows = slice(s * TOKEN_TILE, (s + 1) * TOKEN_TILE)
        x = x_ref[rows, :]
        n = _rms_norm(x, gain_ref[...]).astype(wdt)
        for j in range(D_FF // FFN_COLS):
            lo = j * FFN_COLS
            gate = jnp.dot(n, wi_ref[:, lo:lo + FFN_COLS], preferred_element_type=F32)
            up = jnp.dot(n, wi_ref[:, D_FF + lo:D_FF + lo + FFN_COLS], preferred_element_type=F32)
            act_ref[s, :, lo:lo + FFN_COLS] = (gate * jax.nn.sigmoid(gate) * up).astype(wdt)
        y = jnp.dot(act_ref[s], wo_ref[...], preferred_element_type=F32)
        h = x + 0.5 * y
        if final_norm:
            h = _rms_norm(h, fgain_ref[...])
        o_ref[rows, :] = h


def _cast_slab_rows(n_rows, n_steps):
    for rows in range(BF16_SUBLANES, n_rows + 1, BF16_SUBLANES):
        if n_rows % rows == 0 and n_rows // rows <= n_steps:
            return rows
    return pl.cdiv(pl.cdiv(n_rows, n_steps), BF16_SUBLANES) * BF16_SUBLANES


def _ffn(x, gain, wi, wo, fgain, *, final_norm, tile, cast=()):
    t = x.shape[0]
    steps = t // tile
    tok = pl.BlockSpec((tile, D_MODEL), lambda i: (i, 0))
    slab_specs = []
    for arr in cast:
        rows = _cast_slab_rows(arr.shape[0], steps)
        last = pl.cdiv(arr.shape[0], rows) - 1
        slab_specs.append(pl.BlockSpec((rows, arr.shape[1]),
                                       lambda i, last=last: (jnp.minimum(i, last), 0)))
    outs = pl.pallas_call(
        functools.partial(_ffn_kernel, final_norm=final_norm, tile=tile, n_cast=len(cast)),
        grid=(steps,),
        in_specs=[tok, _resident((1, D_MODEL)), _resident(wi.shape), _resident(wo.shape),
                  _resident((1, D_MODEL))] + slab_specs,
        out_specs=[tok] + slab_specs,
        out_shape=[jax.ShapeDtypeStruct((t, D_MODEL), F32)]
                  + [jax.ShapeDtypeStruct(arr.shape, BF16) for arr in cast],
        scratch_shapes=[pltpu.VMEM((tile // TOKEN_TILE, TOKEN_TILE, D_FF), wi.dtype)],
        compiler_params=_params(),
        name="ffn_final" if final_norm else "ffn",
    )(x, gain, wi, wo, fgain, *cast)
    return outs[0], tuple(outs[1:])


def _split_bf16(x):
    parts = []
    for _ in range(N_SPLIT):
        p = x.astype(BF16).astype(F32)
        parts.append(p)
        x = x - p
    return parts


def _proj_kernel(h_ref, gain_ref, wqt_ref, wvt_ref, wnt_ref, bf_ref, lng_ref, lnb_ref,
                 wsp_ref, bsp_ref,
                 q_ref, kx_ref, v_ref, sgu_ref, carry_ref, *, tiles_per_seq):
    i = pl.program_id(0)

    @pl.when(i % tiles_per_seq == 0)
    def _():
        carry_ref[...] = jnp.zeros_like(carry_ref)

    tdims = (((1,), (1,)), ((), ()))
    sub = ATTN_BLOCK
    sel_row = lax.broadcasted_iota(jnp.int32, (QK_DEPTH - HEAD_DIM, sub), 0)
    ones_rows = jnp.where(sel_row < N_SPLIT, 1.0, 0.0).astype(BF16)
    pad_row = lax.broadcasted_iota(jnp.int32, (V_ROWS - HEAD_DIM, sub), 0)
    ones_row = jnp.where(pad_row == 0, 1.0, 0.0).astype(BF16)
    lane = lax.broadcasted_iota(jnp.int32, (CHUNK, LANES), 1)
    term = lane % N_SPLIT
    live = lane < N_SPLIT * N_HEADS

    def by_term(parts):
        out = parts[-1]
        for j in range(N_SPLIT - 2, -1, -1):
            out = jnp.where(term == j, parts[j], out)
        return out

    row = lax.broadcasted_iota(jnp.int32, (CHUNK, CHUNK), 0)
    col = lax.broadcasted_iota(jnp.int32, (CHUNK, CHUNK), 1)
    tri = jnp.where(row >= col, 1.0, 0.0).astype(BF16)
    pack_w = SGU_PACK * GROUP_DIM
    wrow = lax.broadcasted_iota(jnp.int32, (CHUNK, SGU_PACK * CHUNK), 0)
    wcol = lax.broadcasted_iota(jnp.int32, (CHUNK, SGU_PACK * CHUNK), 1) % CHUNK
    wsp = [jnp.where(wcol <= wrow, wsp_ref[p], 0.0).astype(BF16)
           for p in range(N_GROUPS // SGU_PACK)]
    lane_group = lax.broadcasted_iota(jnp.int32, (CHUNK, pack_w), 1) // GROUP_DIM

    def project(s):
        r0 = s * sub
        nb = _rms_norm(h_ref[r0:r0 + sub, :], gain_ref[...]).astype(BF16)
        qt = lax.dot_general(wqt_ref[...], nb, tdims, preferred_element_type=F32)
        qt = (qt * LOG2E).astype(BF16)
        vt = lax.dot_general(wvt_ref[...], nb, tdims, preferred_element_type=F32).astype(BF16)
        for hd in range(N_HEADS):
            q_ref[0, hd, :HEAD_DIM, r0:r0 + sub] = qt[hd * HEAD_DIM:(hd + 1) * HEAD_DIM, :]
            q_ref[0, hd, HEAD_DIM:, r0:r0 + sub] = ones_rows
            v_ref[0, hd, s, :HEAD_DIM, :] = vt[hd * HEAD_DIM:(hd + 1) * HEAD_DIM, :]
            v_ref[0, hd, s, HEAD_DIM:, :] = ones_row
        return lax.dot_general(nb, wnt_ref[...], tdims, preferred_element_type=F32)

    def tail(s, z, carry):
        r0 = s * sub
        zk = z[:, :ATTN_WIDTH]
        u = z[:, ATTN_WIDTH:ATTN_WIDTH + SGU_WIDTH]
        vg = z[:, ATTN_WIDTH + SGU_WIDTH:ATTN_WIDTH + 2 * SGU_WIDTH]
        f = z[:, ATTN_WIDTH + 2 * SGU_WIDTH:] + bf_ref[...]
        logf = jnp.minimum(f, 0.0) - jnp.log1p(jnp.exp(-jnp.abs(f)))

        gu = _gelu(u)
        gv = _gelu(vg)
        mu = jnp.mean(gv, axis=-1, keepdims=True)
        dv = gv - mu
        var = jnp.mean(dv * dv, axis=-1, keepdims=True)
        vn = ((dv * lax.rsqrt(var + LN_EPS)) * lng_ref[...] + lnb_ref[...]).astype(BF16)

        for c in range(sub // CHUNK):
            rows = slice(c * CHUNK, (c + 1) * CHUNK)
            out_rows = slice(r0 + c * CHUNK, r0 + (c + 1) * CHUNK)
            local = jnp.dot(tri, by_term(_split_bf16(logf[rows])).astype(BF16),
                            preferred_element_type=F32)
            shifted = {0: local}
            for d in range(1, N_SPLIT):
                shifted[d] = pltpu.roll(local, d, 1)
                shifted[-d] = pltpu.roll(local, LANES - d, 1)
            csum = carry
            for j in range(N_SPLIT):
                csum = csum + by_term([shifted[t - j] for t in range(N_SPLIT)])
            carry = csum[CHUNK - 1:, :]
            bias = jnp.where(live, by_term(_split_bf16(csum * -LOG2E)), 0.0)
            for hd in range(N_HEADS):
                pair = hd // 2
                kslab = zk[rows, pair * LANES:(pair + 1) * LANES]
                if hd % 2:
                    kslab = pltpu.roll(kslab, HEAD_DIM, 1)
                hbias = pltpu.roll(bias, HEAD_DIM - N_SPLIT * hd, 1)
                kx = jnp.where(lane < HEAD_DIM, kslab,
                               jnp.where(lane < HEAD_DIM + N_SPLIT, hbias, 0.0))
                kx_ref[0, hd, out_rows, :] = kx.astype(BF16)

            mixed = []
            for p in range(N_GROUPS // SGU_PACK):
                slab = vn[rows, p * pack_w:(p + 1) * pack_w]
                blocks = jnp.concatenate(
                    [jnp.where(lane_group == r, slab, jnp.zeros_like(slab))
                     for r in range(SGU_PACK)], axis=0)
                mixed.append(jnp.dot(wsp[p], blocks, preferred_element_type=F32))
            mixed = jnp.concatenate(mixed, axis=1) + bsp_ref[...]
            sgu_ref[out_rows, :] = (gu[rows] * mixed).astype(BF16)
        return carry

    n_sub = PROJ_TILE // sub
    carry = carry_ref[...]
    z_prev = project(0)
    for s in range(1, n_sub):
        z = project(s)
        carry = tail(s - 1, z_prev, carry)
        z_prev = z
    carry_ref[...] = tail(n_sub - 1, z_prev, carry)


def _proj(h, gain, wqt, wvt, wnt, bf, lng, lnb, wsp, bsp, *, batch, seq):
    t = batch * seq
    tiles_per_seq = seq // PROJ_TILE
    sub = PROJ_TILE // ATTN_BLOCK
    tok = lambda i: (i, 0)
    return pl.pallas_call(
        functools.partial(_proj_kernel, tiles_per_seq=tiles_per_seq),
        grid=(t // PROJ_TILE,),
        in_specs=[pl.BlockSpec((PROJ_TILE, D_MODEL), tok), _resident((1, D_MODEL)),
                  _resident(wqt.shape), _resident(wvt.shape), _resident(wnt.shape),
                  _resident((1, LANES)), _resident((1, SGU_WIDTH)), _resident((1, SGU_WIDTH)),
                  _resident(wsp.shape), _resident(bsp.shape)],
        out_specs=[pl.BlockSpec((1, N_HEADS, QK_DEPTH, PROJ_TILE),
                                lambda i: (i // tiles_per_seq, 0, 0, i % tiles_per_seq)),
                   pl.BlockSpec((1, N_HEADS, PROJ_TILE, LANES),
                                lambda i: (i // tiles_per_seq, 0, i % tiles_per_seq, 0)),
                   pl.BlockSpec((1, N_HEADS, sub, V_ROWS, ATTN_BLOCK),
                                lambda i: (i // tiles_per_seq, 0, i % tiles_per_seq, 0, 0)),
                   pl.BlockSpec((PROJ_TILE, SGU_WIDTH), tok)],
        out_shape=[jax.ShapeDtypeStruct((batch, N_HEADS, QK_DEPTH, seq), BF16),
                   jax.ShapeDtypeStruct((batch, N_HEADS, seq, LANES), BF16),
                   jax.ShapeDtypeStruct((batch, N_HEADS, seq // ATTN_BLOCK, V_ROWS, ATTN_BLOCK), BF16),
                   jax.ShapeDtypeStruct((t, SGU_WIDTH), BF16)],
        scratch_shapes=[pltpu.VMEM((1, LANES), F32)],
        compiler_params=_params(),
        name="proj",
    )(h, gain, wqt, wvt, wnt, bf, lng, lnb, wsp, bsp)


def _attn_kernel(q_ref, qnext_ref, kx_ref, v_ref, o_ref, s_ref, smax_ref, m_ref, acc_ref, *, nblk):
    blk = ATTN_BLOCK
    n_diag = Q_TILE // blk
    qi = pl.program_id(1)
    n_off = qi * n_diag

    m_ref[...] = jnp.full(m_ref.shape, -jnp.inf, F32)
    acc_ref[...] = jnp.zeros(acc_ref.shape, F32)

    key_pos = lax.broadcasted_iota(jnp.int32, (blk, Q_TILE), 0)
    qry_pos = lax.broadcasted_iota(jnp.int32, (blk, Q_TILE), 1)
    causal = key_pos <= qry_pos

    def scores(j, hd, diag, qt_ref=q_ref):
        rows = pl.ds(j * blk if isinstance(j, int) else pl.multiple_of(j * blk, blk), blk)
        lo = 0 if diag is None else diag * blk
        st = jnp.dot(kx_ref[0, hd, rows, :], qt_ref[0, hd, :, lo:], preferred_element_type=F32)
        if diag is not None:
            st = jnp.where(causal[:, :Q_TILE - lo], st, -jnp.inf)
        s_ref[hd, :, lo:] = st
        smax_ref[hd, :, lo:] = jnp.max(st, axis=0, keepdims=True)

    def step(j, j_next, next_diag=None, next_q_ref=q_ref, lo=0):
        for hd in range(N_HEADS):
            st = s_ref[hd, :, lo:]
            m_old = m_ref[hd, :, lo:]
            m_new = jnp.maximum(m_old, smax_ref[hd, :, lo:])
            scores(j_next, hd, next_diag, next_q_ref)
            alpha = jnp.exp2(m_old - m_new)
            pt = jnp.exp2(st - m_new).astype(BF16)
            pv = jnp.dot(v_ref[0, hd, j], pt, preferred_element_type=F32)
            acc_ref[hd, :, lo:] = acc_ref[hd, :, lo:] * alpha + pv
            m_ref[hd, :, lo:] = m_new

    @pl.when(qi == 0)
    def _():
        for hd in range(N_HEADS):
            scores(n_off, hd, 0)

    for d in range(1, n_diag):
        step(n_off + d - 1, n_off + d, d, lo=(d - 1) * blk)
    last_diag = n_off + n_diag - 1
    last_lo = (n_diag - 1) * blk

    @pl.when(qi == 0)
    def _():
        step(last_diag, min(n_diag, nblk - n_diag), 0, qnext_ref, lo=last_lo)

    def group(i):
        for d in range(n_diag):
            step(i * n_diag + d - 1, i * n_diag + d)

    def body(t, carry):
        for g in range(LOOP_GROUPS):
            group(1 + t * LOOP_GROUPS + g)
        return carry

    @pl.when(qi > 0)
    def _():
        step(last_diag, 0, lo=last_lo)
        for d in range(1, n_diag):
            step(d - 1, d)
        n_body = (qi - 1) // LOOP_GROUPS
        lax.fori_loop(0, n_body, body, 0)
        for g in range(LOOP_GROUPS - 1):
            @pl.when(n_body * LOOP_GROUPS + g < qi - 1)
            def _():
                group(1 + n_body * LOOP_GROUPS + g)
        step(n_off - 1, jnp.minimum(n_off + n_diag, nblk - n_diag), 0, qnext_ref)

    for hd in range(N_HEADS):
        acc = acc_ref[hd]
        o_ref[0, hd * HEAD_DIM:(hd + 1) * HEAD_DIM, :] = (
            acc[:HEAD_DIM] / acc[HEAD_DIM:HEAD_DIM + 1]).astype(BF16)


def _attn(q, kx, v, *, batch, seq):
    nblk = seq // ATTN_BLOCK
    n_tiles = seq // Q_TILE
    q_spec = lambda tile: pl.BlockSpec((1, N_HEADS, QK_DEPTH, Q_TILE),
                                       lambda b, i: (b, 0, 0, tile(i)))
    return pl.pallas_call(
        functools.partial(_attn_kernel, nblk=nblk),
        grid=(batch, n_tiles),
        in_specs=[q_spec(lambda i: i), q_spec(lambda i: jnp.minimum(i + 1, n_tiles - 1)),
                  pl.BlockSpec((1, N_HEADS, seq, LANES), lambda b, i: (b, 0, 0, 0)),
                  pl.BlockSpec((1, N_HEADS, nblk, V_ROWS, ATTN_BLOCK), lambda b, i: (b, 0, 0, 0, 0))],
        out_specs=pl.BlockSpec((1, ATTN_WIDTH, Q_TILE), lambda b, i: (b, 0, i)),
        out_shape=jax.ShapeDtypeStruct((batch, ATTN_WIDTH, seq), BF16),
        scratch_shapes=[pltpu.VMEM((N_HEADS, ATTN_BLOCK, Q_TILE), F32),
                        pltpu.VMEM((N_HEADS, 1, Q_TILE), F32),
                        pltpu.VMEM((N_HEADS, 1, Q_TILE), F32),
                        pltpu.VMEM((N_HEADS, V_ROWS, Q_TILE), F32)],
        compiler_params=pltpu.CompilerParams(dimension_semantics=("arbitrary", "arbitrary"),
                                             vmem_limit_bytes=VMEM_LIMIT),
        name="attn",
    )(q, q, kx, v)


def _merge_kernel(h_ref, at_ref, sgu_ref, gain_ref, wgatet_ref, bgate_ref, wua_ref, wug_ref,
                  wout_ref, o_ref):
    for s in range(MERGE_TILE // TOKEN_TILE):
        rows = slice(s * TOKEN_TILE, (s + 1) * TOKEN_TILE)
        h = h_ref[rows, :]
        nb = _rms_norm(h, gain_ref[...]).astype(BF16)
        logits = lax.dot_general(nb, wgatet_ref[...], (((1,), (1,)), ((), ())),
                                 preferred_element_type=F32)
        gates = jax.nn.sigmoid(logits + bgate_ref[...])
        up_a = lax.dot_general(at_ref[0, :, rows], wua_ref[...], (((0,), (0,)), ((), ())),
                               preferred_element_type=F32)
        up_g = jnp.dot(sgu_ref[rows, :], wug_ref[...], preferred_element_type=F32)
        merged = gates[:, :D_MODEL] * up_a + gates[:, D_MODEL:] * up_g
        o_ref[rows, :] = h + jnp.dot(merged.astype(BF16), wout_ref[...],
                                     preferred_element_type=F32)


def _merge(h, at, sgu, gain, wgate, bgate, wua, wug, wout, *, seq):
    t = h.shape[0]
    tiles_per_seq = seq // MERGE_TILE
    tok = lambda i: (i, 0)
    return pl.pallas_call(
        _merge_kernel,
        grid=(t // MERGE_TILE,),
        in_specs=[pl.BlockSpec((MERGE_TILE, D_MODEL), tok),
                  pl.BlockSpec((1, ATTN_WIDTH, MERGE_TILE),
                               lambda i: (i // tiles_per_seq, 0, i % tiles_per_seq)),
                  pl.BlockSpec((MERGE_TILE, SGU_WIDTH), tok),
                  _resident((1, D_MODEL)), _resident(wgate.shape), _resident(bgate.shape),
                  _resident(wua.shape), _resident(wug.shape), _resident(wout.shape)],
        out_specs=pl.BlockSpec((MERGE_TILE, D_MODEL), tok),
        out_shape=jax.ShapeDtypeStruct((t, D_MODEL), F32),
        compiler_params=_params(),
        name="merge",
    )(h, at, sgu, gain, wgate, bgate, wua, wug, wout)


def kernel(x, ffn1_norm, ffn1_w_in, ffn1_w_out, mix_norm, w_in, b_forget, b_gate, sgu_ln_gain,
           sgu_ln_bias, sgu_w_spatial, sgu_b_spatial, w_up_attn, w_up_sgu, w_out, ffn2_norm,
           ffn2_w_in, ffn2_w_out, final_norm):
    batch, seq, _ = x.shape
    depth = ffn1_norm.shape[0]
    row = lambda v: v.reshape(1, -1)
    fgain = row(final_norm)
    h = x.reshape(batch * seq, D_MODEL)
    for l in range(depth):
        last = l == depth - 1
        h, (wi2, wo2, wtb, wua, wug, wout) = _ffn(
            h, row(ffn1_norm[l]), ffn1_w_in[l], ffn1_w_out[l], fgain, final_norm=False,
            tile=TOKEN_TILE, cast=(ffn2_w_in[l], ffn2_w_out[l], w_in[l].T, w_up_attn[l],
                                   w_up_sgu[l], w_out[l]))

        o_k, o_v, o_f = ATTN_WIDTH, 2 * ATTN_WIDTH, 3 * ATTN_WIDTH
        o_u = o_f + N_HEADS
        o_vg, o_g = o_u + SGU_WIDTH, o_u + 2 * SGU_WIDTH
        wt = wtb
        wqt = wt[:o_k] * jnp.asarray(HEAD_DIM ** -0.5, BF16)
        wvt = wt[o_v:o_f]
        wft = jnp.repeat(wt[o_f:o_u], N_SPLIT, axis=0)
        wft = jnp.pad(wft, ((0, LANES - N_SPLIT * N_HEADS), (0, 0)))
        wnt = jnp.concatenate([wt[o_k:o_v], wt[o_u:o_g], wft], axis=0)
        bf = jnp.pad(jnp.repeat(b_forget[l], N_SPLIT), (0, LANES - N_SPLIT * N_HEADS)).reshape(1, LANES)
        wsp = sgu_w_spatial[l].reshape(N_GROUPS // SGU_PACK, SGU_PACK, CHUNK, CHUNK)
        wsp = wsp.transpose(0, 2, 1, 3).reshape(N_GROUPS // SGU_PACK, CHUNK, SGU_PACK * CHUNK)
        bsp = jnp.repeat(sgu_b_spatial[l].T, GROUP_DIM, axis=1)

        q, kx, v, sgu = _proj(h, row(mix_norm[l]), wqt, wvt, wnt, bf, row(sgu_ln_gain[l]),
                              row(sgu_ln_bias[l]), wsp, bsp, batch=batch, seq=seq)
        at = _attn(q, kx, v, batch=batch, seq=seq)
        h = _merge(h, at, sgu, row(mix_norm[l]), wt[o_g:], row(b_gate[l]), wua, wug, wout,
                   seq=seq)
        h, _ = _ffn(h, row(ffn2_norm[l]), wi2, wo2, fgain, final_norm=last, tile=FFN2_TILE)
    return h.reshape(batch, seq, D_MODEL)
```

```python
import functools

import jax
import jax.numpy as jnp
from jax import lax
from jax.experimental import pallas as pl
from jax.experimental.pallas import tpu as pltpu

F32 = jnp.float32
BF16 = jnp.bfloat16

D_MODEL = 1024
D_FF = 2816
N_HEADS = 8
HEAD_DIM = 64
ATTN_WIDTH = N_HEADS * HEAD_DIM
N_GROUPS = 8
GROUP_DIM = 64
SGU_WIDTH = N_GROUPS * GROUP_DIM
CHUNK = 128
RMS_EPS = 1e-6
LN_EPS = 1e-5
LOG2E = 1.4426950408889634

LANES = 128
BF16_SUBLANES = 16
F32_SUBLANES = 8
V7X_VMEM_BYTES = 64 * 1024 * 1024
VMEM_LIMIT = V7X_VMEM_BYTES * 7 // 8

TOKEN_TILE = 512
PROJ_TILE = 1024
MERGE_TILE = 1024
FFN2_TILE = 1024
FFN_COLS = 256
ATTN_BLOCK = 256
Q_TILE = 512
LOOP_GROUPS = 2
V_ROWS = HEAD_DIM + 16
N_SPLIT = 3
QK_DEPTH = LANES
SGU_PACK = 4


def _rms_norm(x, gain):
    return (x * lax.rsqrt(jnp.mean(x * x, axis=-1, keepdims=True) + RMS_EPS)) * gain


def _gelu(x):
    return x * (lax.erf(x * (2.0 ** -0.5)) + 1.0) * 0.5


def _resident(shape):
    zeros = (0,) * len(shape)
    return pl.BlockSpec(shape, lambda *_: zeros, pipeline_mode=pl.Buffered(1))


def _params():
    return pltpu.CompilerParams(dimension_semantics=("arbitrary",),
                                vmem_limit_bytes=VMEM_LIMIT)


def _ffn_kernel(*refs, final_norm, tile, shifts):
    x_ref, gain_ref, wi_ref, wo_ref, fgain_ref = refs[:5]
    n_src = sum(2 if shift else 1 for shift in shifts)
    cast_src = list(refs[5:5 + n_src])
    o_ref = refs[5 + n_src]
    cast_dst = refs[6 + n_src:6 + n_src + len(shifts)]
    act_ref = refs[6 + n_src + len(shifts)]

    for shift, dst in zip(shifts, cast_dst):
        slab = cast_src.pop(0)[...]
        if shift:
            slab = jnp.concatenate([slab[shift:], cast_src.pop(0)[:shift, :]], axis=0)
        dst[...] = slab.astype(dst.dtype)

    wdt = wi_ref.dtype
    for s in range(tile // TOKEN_TILE):
        rows = slice(s * TOKEN_TILE, (s + 1) * TOKEN_TILE)
        x = x_ref[rows, :]
        n = _rms_norm(x, gain_ref[...]).astype(wdt)
        for j in range(D_FF // FFN_COLS):
            lo = j * FFN_COLS
            gate = jnp.dot(n, wi_ref[:, lo:lo + FFN_COLS], preferred_element_type=F32)
            up = jnp.dot(n, wi_ref[:, D_FF + lo:D_FF + lo + FFN_COLS], preferred_element_type=F32)
            act_ref[s, :, lo:lo + FFN_COLS] = (gate * jax.nn.sigmoid(gate) * up).astype(wdt)
        y = jnp.dot(act_ref[s], wo_ref[...], preferred_element_type=F32)
        h = x + 0.5 * y
        if final_norm:
            h = _rms_norm(h, fgain_ref[...])
        o_ref[rows, :] = h


def _cast_slab_rows(n_rows, n_steps):
    for rows in range(BF16_SUBLANES, n_rows + 1, BF16_SUBLANES):
        if n_rows % rows == 0 and n_rows // rows <= n_steps:
            return rows
    return pl.cdiv(pl.cdiv(n_rows, n_steps), BF16_SUBLANES) * BF16_SUBLANES


def _ffn(x, gain, wi, wo, fgain, *, final_norm, tile, cast=()):
    t = x.shape[0]
    steps = t // tile
    tok = pl.BlockSpec((tile, D_MODEL), lambda i: (i, 0))
    srcs, src_specs, dst_specs, dst_shapes, shifts = [], [], [], [], []
    for job in cast:
        arr, row0, n_rows = job if isinstance(job, tuple) else (job, 0, job.shape[0])
        rows = _cast_slab_rows(n_rows, steps)
        last = pl.cdiv(n_rows, rows) - 1
        first, shift = divmod(row0, rows)
        assert not shift or (shift % F32_SUBLANES == 0 and arr.dtype == F32), (row0, rows)
        for block in range(first, first + (2 if shift else 1)):
            srcs.append(arr)
            src_specs.append(pl.BlockSpec(
                (rows, arr.shape[1]),
                lambda i, last=last, block=block: (block + jnp.minimum(i, last), 0)))
        dst_specs.append(pl.BlockSpec((rows, arr.shape[1]),
                                      lambda i, last=last: (jnp.minimum(i, last), 0)))
        dst_shapes.append(jax.ShapeDtypeStruct((n_rows, arr.shape[1]), BF16))
        shifts.append(shift)
    outs = pl.pallas_call(
        functools.partial(_ffn_kernel, final_norm=final_norm, tile=tile, shifts=tuple(shifts)),
        grid=(steps,),
        in_specs=[tok, _resident((1, D_MODEL)), _resident(wi.shape), _resident(wo.shape),
                  _resident((1, D_MODEL))] + src_specs,
        out_specs=[tok] + dst_specs,
        out_shape=[jax.ShapeDtypeStruct((t, D_MODEL), F32)] + dst_shapes,
        scratch_shapes=[pltpu.VMEM((tile // TOKEN_TILE, TOKEN_TILE, D_FF), wi.dtype)],
        compiler_params=_params(),
        name="ffn_final" if final_norm else "ffn",
    )(x, gain, wi, wo, fgain, *srcs)
    return outs[0], tuple(outs[1:])


def _split_bf16(x):
    parts = []
    for _ in range(N_SPLIT):
        p = x.astype(BF16).astype(F32)
        parts.append(p)
        x = x - p
    return parts


def _proj_kernel(h_ref, gain_ref, wqt_ref, wkt_ref, wvt_ref, wnt_ref, bf_ref, lng_ref, lnb_ref,
                 wsp_ref, bsp_ref,
                 q_ref, kx_ref, v_ref, sgu_ref, carry_ref, *, tiles_per_seq):
    i = pl.program_id(0)

    @pl.when(i % tiles_per_seq == 0)
    def _():
        carry_ref[...] = jnp.zeros_like(carry_ref)

    tdims = (((1,), (1,)), ((), ()))
    sub = ATTN_BLOCK
    sel_row = lax.broadcasted_iota(jnp.int32, (QK_DEPTH - HEAD_DIM, sub), 0)
    ones_rows = jnp.where(sel_row < N_SPLIT, 1.0, 0.0).astype(BF16)
    pad_row = lax.broadcasted_iota(jnp.int32, (V_ROWS - HEAD_DIM, sub), 0)
    ones_row = jnp.where(pad_row == 0, 1.0, 0.0).astype(BF16)
    lane = lax.broadcasted_iota(jnp.int32, (CHUNK, LANES), 1)
    term = lane % N_SPLIT
    live = lane < N_SPLIT * N_HEADS

    def by_term(parts):
        out = parts[-1]
        for j in range(N_SPLIT - 2, -1, -1):
            out = jnp.where(term == j, parts[j], out)
        return out

    row = lax.broadcasted_iota(jnp.int32, (CHUNK, CHUNK), 0)
    col = lax.broadcasted_iota(jnp.int32, (CHUNK, CHUNK), 1)
    tri = jnp.where(row >= col, 1.0, 0.0).astype(BF16)
    pack_w = SGU_PACK * GROUP_DIM
    wrow = lax.broadcasted_iota(jnp.int32, (CHUNK, SGU_PACK * CHUNK), 0)
    wcol = lax.broadcasted_iota(jnp.int32, (CHUNK, SGU_PACK * CHUNK), 1) % CHUNK
    wsp = [jnp.where(wcol <= wrow, wsp_ref[p], 0.0).astype(BF16)
           for p in range(N_GROUPS // SGU_PACK)]
    lane_group = lax.broadcasted_iota(jnp.int32, (CHUNK, pack_w), 1) // GROUP_DIM

    def project(s):
        r0 = s * sub
        nb = _rms_norm(h_ref[r0:r0 + sub, :], gain_ref[...]).astype(BF16)
        qt = lax.dot_general(wqt_ref[...], nb, tdims, preferred_element_type=F32)
        qt = (qt * (LOG2E * HEAD_DIM ** -0.5)).astype(BF16)
        vt = lax.dot_general(wvt_ref[...], nb, tdims, preferred_element_type=F32).astype(BF16)
        for hd in range(N_HEADS):
            q_ref[0, hd, :HEAD_DIM, r0:r0 + sub] = qt[hd * HEAD_DIM:(hd + 1) * HEAD_DIM, :]
            q_ref[0, hd, HEAD_DIM:, r0:r0 + sub] = ones_rows
            v_ref[0, hd, s, :HEAD_DIM, :] = vt[hd * HEAD_DIM:(hd + 1) * HEAD_DIM, :]
            v_ref[0, hd, s, HEAD_DIM:, :] = ones_row
        return (lax.dot_general(nb, wkt_ref[...], tdims, preferred_element_type=F32),
                lax.dot_general(nb, wnt_ref[...], tdims, preferred_element_type=F32))

    def tail(s, zs, carry):
        r0 = s * sub
        zk, z = zs
        u = z[:, :SGU_WIDTH]
        vg = z[:, SGU_WIDTH:2 * SGU_WIDTH]
        f = z[:, 2 * SGU_WIDTH:] + bf_ref[...]
        logf = jnp.minimum(f, 0.0) - jnp.log1p(jnp.exp(-jnp.abs(f)))

        gu = _gelu(u)
        gv = _gelu(vg)
        mu = jnp.mean(gv, axis=-1, keepdims=True)
        dv = gv - mu
        var = jnp.mean(dv * dv, axis=-1, keepdims=True)
        vn = ((dv * lax.rsqrt(var + LN_EPS)) * lng_ref[...] + lnb_ref[...]).astype(BF16)

        for c in range(sub // CHUNK):
            rows = slice(c * CHUNK, (c + 1) * CHUNK)
            out_rows = slice(r0 + c * CHUNK, r0 + (c + 1) * CHUNK)
            local = jnp.dot(tri, by_term(_split_bf16(logf[rows])).astype(BF16),
                            preferred_element_type=F32)
            shifted = {0: local}
            for d in range(1, N_SPLIT):
                shifted[d] = pltpu.roll(local, d, 1)
                shifted[-d] = pltpu.roll(local, LANES - d, 1)
            csum = carry
            for j in range(N_SPLIT):
                csum = csum + by_term([shifted[t - j] for t in range(N_SPLIT)])
            carry = csum[CHUNK - 1:, :]
            bias = jnp.where(live, by_term(_split_bf16(csum * -LOG2E)), 0.0)
            for hd in range(N_HEADS):
                pair = hd // 2
                kslab = zk[rows, pair * LANES:(pair + 1) * LANES]
                if hd % 2:
                    kslab = pltpu.roll(kslab, HEAD_DIM, 1)
                hbias = pltpu.roll(bias, HEAD_DIM - N_SPLIT * hd, 1)
                kx = jnp.where(lane < HEAD_DIM, kslab,
                               jnp.where(lane < HEAD_DIM + N_SPLIT, hbias, 0.0))
                kx_ref[0, hd, out_rows, :] = kx.astype(BF16)

            mixed = []
            for p in range(N_GROUPS // SGU_PACK):
                slab = vn[rows, p * pack_w:(p + 1) * pack_w]
                blocks = jnp.concatenate(
                    [jnp.where(lane_group == r, slab, jnp.zeros_like(slab))
                     for r in range(SGU_PACK)], axis=0)
                mixed.append(jnp.dot(wsp[p], blocks, preferred_element_type=F32))
            mixed = jnp.concatenate(mixed, axis=1) + bsp_ref[...]
            sgu_ref[out_rows, :] = (gu[rows] * mixed).astype(BF16)
        return carry

    n_sub = PROJ_TILE // sub
    carry = carry_ref[...]
    z_prev = project(0)
    for s in range(1, n_sub):
        z = project(s)
        carry = tail(s - 1, z_prev, carry)
        z_prev = z
    carry_ref[...] = tail(n_sub - 1, z_prev, carry)


def _proj(h, gain, wt, wnt, bf, lng, lnb, wsp, bsp, *, batch, seq):
    t = batch * seq
    tiles_per_seq = seq // PROJ_TILE
    sub = PROJ_TILE // ATTN_BLOCK
    tok = lambda i: (i, 0)
    wblock = lambda n: pl.BlockSpec((ATTN_WIDTH, D_MODEL), lambda i: (n, 0),
                                    pipeline_mode=pl.Buffered(1))
    return pl.pallas_call(
        functools.partial(_proj_kernel, tiles_per_seq=tiles_per_seq),
        grid=(t // PROJ_TILE,),
        in_specs=[pl.BlockSpec((PROJ_TILE, D_MODEL), tok), _resident((1, D_MODEL)),
                  wblock(0), wblock(1), wblock(2), _resident(wnt.shape),
                  _resident((1, LANES)), _resident((1, SGU_WIDTH)), _resident((1, SGU_WIDTH)),
                  _resident(wsp.shape), _resident(bsp.shape)],
        out_specs=[pl.BlockSpec((1, N_HEADS, QK_DEPTH, PROJ_TILE),
                                lambda i: (i // tiles_per_seq, 0, 0, i % tiles_per_seq)),
                   pl.BlockSpec((1, N_HEADS, PROJ_TILE, LANES),
                                lambda i: (i // tiles_per_seq, 0, i % tiles_per_seq, 0)),
                   pl.BlockSpec((1, N_HEADS, sub, V_ROWS, ATTN_BLOCK),
                                lambda i: (i // tiles_per_seq, 0, i % tiles_per_seq, 0, 0)),
                   pl.BlockSpec((PROJ_TILE, SGU_WIDTH), tok)],
        out_shape=[jax.ShapeDtypeStruct((batch, N_HEADS, QK_DEPTH, seq), BF16),
                   jax.ShapeDtypeStruct((batch, N_HEADS, seq, LANES), BF16),
                   jax.ShapeDtypeStruct((batch, N_HEADS, seq // ATTN_BLOCK, V_ROWS, ATTN_BLOCK), BF16),
                   jax.ShapeDtypeStruct((t, SGU_WIDTH), BF16)],
        scratch_shapes=[pltpu.VMEM((1, LANES), F32)],
        compiler_params=_params(),
        name="proj",
    )(h, gain, wt, wt, wt, wnt, bf, lng, lnb, wsp, bsp)


def _attn_kernel(q_ref, qnext_ref, kx_ref, v_ref, o_ref, s_ref, smax_ref, m_ref, acc_ref, *, nblk):
    blk = ATTN_BLOCK
    n_diag = Q_TILE // blk
    qi = pl.program_id(1)
    n_off = qi * n_diag

    m_ref[...] = jnp.full(m_ref.shape, -jnp.inf, F32)
    acc_ref[...] = jnp.zeros(acc_ref.shape, F32)

    key_pos = lax.broadcasted_iota(jnp.int32, (blk, Q_TILE), 0)
    qry_pos = lax.broadcasted_iota(jnp.int32, (blk, Q_TILE), 1)
    causal = key_pos <= qry_pos

    def scores(j, hd, diag, qt_ref=q_ref):
        rows = pl.ds(j * blk if isinstance(j, int) else pl.multiple_of(j * blk, blk), blk)
        lo = 0 if diag is None else diag * blk
        st = jnp.dot(kx_ref[0, hd, rows, :], qt_ref[0, hd, :, lo:], preferred_element_type=F32)
        if diag is not None:
            st = jnp.where(causal[:, :Q_TILE - lo], st, -jnp.inf)
        s_ref[hd, :, lo:] = st
        smax_ref[hd, :, lo:] = jnp.max(st, axis=0, keepdims=True)

    def step(j, j_next, next_diag=None, next_q_ref=q_ref, lo=0):
        for hd in range(N_HEADS):
            st = s_ref[hd, :, lo:]
            m_old = m_ref[hd, :, lo:]
            m_new = jnp.maximum(m_old, smax_ref[hd, :, lo:])
            scores(j_next, hd, next_diag, next_q_ref)
            alpha = jnp.exp2(m_old - m_new)
            pt = jnp.exp2(st - m_new).astype(BF16)
            pv = jnp.dot(v_ref[0, hd, j], pt, preferred_element_type=F32)
            acc_ref[hd, :, lo:] = acc_ref[hd, :, lo:] * alpha + pv
            m_ref[hd, :, lo:] = m_new

    @pl.when(qi == 0)
    def _():
        for hd in range(N_HEADS):
            scores(n_off, hd, 0)

    for d in range(1, n_diag):
        step(n_off + d - 1, n_off + d, d, lo=(d - 1) * blk)
    last_diag = n_off + n_diag - 1
    last_lo = (n_diag - 1) * blk

    @pl.when(qi == 0)
    def _():
        step(last_diag, min(n_diag, nblk - n_diag), 0, qnext_ref, lo=last_lo)

    def group(i):
        for d in range(n_diag):
            step(i * n_diag + d - 1, i * n_diag + d)

    def body(t, carry):
        for g in range(LOOP_GROUPS):
            group(1 + t * LOOP_GROUPS + g)
        return carry

    @pl.when(qi > 0)
    def _():
        step(last_diag, 0, lo=last_lo)
        for d in range(1, n_diag):
            step(d - 1, d)
        n_body = (qi - 1) // LOOP_GROUPS
        lax.fori_loop(0, n_body, body, 0)
        for g in range(LOOP_GROUPS - 1):
            @pl.when(n_body * LOOP_GROUPS + g < qi - 1)
            def _():
                group(1 + n_body * LOOP_GROUPS + g)
        step(n_off - 1, jnp.minimum(n_off + n_diag, nblk - n_diag), 0, qnext_ref)

    for hd in range(N_HEADS):
        acc = acc_ref[hd]
        o_ref[0, hd * HEAD_DIM:(hd + 1) * HEAD_DIM, :] = (
            acc[:HEAD_DIM] / acc[HEAD_DIM:HEAD_DIM + 1]).astype(BF16)


def _attn(q, kx, v, *, batch, seq):
    nblk = seq // ATTN_BLOCK
    n_tiles = seq // Q_TILE
    q_spec = lambda tile: pl.BlockSpec((1, N_HEADS, QK_DEPTH, Q_TILE),
                                       lambda b, i: (b, 0, 0, tile(i)))
    return pl.pallas_call(
        functools.partial(_attn_kernel, nblk=nblk),
        grid=(batch, n_tiles),
        in_specs=[q_spec(lambda i: i), q_spec(lambda i: jnp.minimum(i + 1, n_tiles - 1)),
                  pl.BlockSpec((1, N_HEADS, seq, LANES), lambda b, i: (b, 0, 0, 0)),
                  pl.BlockSpec((1, N_HEADS, nblk, V_ROWS, ATTN_BLOCK), lambda b, i: (b, 0, 0, 0, 0))],
        out_specs=pl.BlockSpec((1, ATTN_WIDTH, Q_TILE), lambda b, i: (b, 0, i)),
        out_shape=jax.ShapeDtypeStruct((batch, ATTN_WIDTH, seq), BF16),
        scratch_shapes=[pltpu.VMEM((N_HEADS, ATTN_BLOCK, Q_TILE), F32),
                        pltpu.VMEM((N_HEADS, 1, Q_TILE), F32),
                        pltpu.VMEM((N_HEADS, 1, Q_TILE), F32),
                        pltpu.VMEM((N_HEADS, V_ROWS, Q_TILE), F32)],
        compiler_params=pltpu.CompilerParams(dimension_semantics=("arbitrary", "arbitrary"),
                                             vmem_limit_bytes=VMEM_LIMIT),
        name="attn",
    )(q, q, kx, v)


def _merge_kernel(h_ref, at_ref, sgu_ref, gain_ref, wgatet_ref, bgate_ref, wua_ref, wug_ref,
                  wout_ref, o_ref):
    for s in range(MERGE_TILE // TOKEN_TILE):
        rows = slice(s * TOKEN_TILE, (s + 1) * TOKEN_TILE)
        h = h_ref[rows, :]
        nb = _rms_norm(h, gain_ref[...]).astype(BF16)
        logits = lax.dot_general(nb, wgatet_ref[...], (((1,), (1,)), ((), ())),
                                 preferred_element_type=F32)
        gates = jax.nn.sigmoid(logits + bgate_ref[...])
        up_a = lax.dot_general(at_ref[0, :, rows], wua_ref[...], (((0,), (0,)), ((), ())),
                               preferred_element_type=F32)
        up_g = jnp.dot(sgu_ref[rows, :], wug_ref[...], preferred_element_type=F32)
        merged = gates[:, :D_MODEL] * up_a + gates[:, D_MODEL:] * up_g
        o_ref[rows, :] = h + jnp.dot(merged.astype(BF16), wout_ref[...],
                                     preferred_element_type=F32)


def _merge(h, at, sgu, gain, wgate, bgate, wua, wug, wout, *, seq):
    t = h.shape[0]
    tiles_per_seq = seq // MERGE_TILE
    tok = lambda i: (i, 0)
    return pl.pallas_call(
        _merge_kernel,
        grid=(t // MERGE_TILE,),
        in_specs=[pl.BlockSpec((MERGE_TILE, D_MODEL), tok),
                  pl.BlockSpec((1, ATTN_WIDTH, MERGE_TILE),
                               lambda i: (i // tiles_per_seq, 0, i % tiles_per_seq)),
                  pl.BlockSpec((MERGE_TILE, SGU_WIDTH), tok),
                  _resident((1, D_MODEL)), _resident(wgate.shape), _resident(bgate.shape),
                  _resident(wua.shape), _resident(wug.shape), _resident(wout.shape)],
        out_specs=pl.BlockSpec((MERGE_TILE, D_MODEL), tok),
        out_shape=jax.ShapeDtypeStruct((t, D_MODEL), F32),
        compiler_params=_params(),
        name="merge",
    )(h, at, sgu, gain, wgate, bgate, wua, wug, wout)


def kernel(x, ffn1_norm, ffn1_w_in, ffn1_w_out, mix_norm, w_in, b_forget, b_gate, sgu_ln_gain,
           sgu_ln_bias, sgu_w_spatial, sgu_b_spatial, w_up_attn, w_up_sgu, w_out, ffn2_norm,
           ffn2_w_in, ffn2_w_out, final_norm):
    batch, seq, _ = x.shape
    depth = ffn1_norm.shape[0]
    row = lambda v: v.reshape(1, -1)
    fgain = row(final_norm)
    h = x.reshape(batch * seq, D_MODEL)
    for l in range(depth):
        last = l == depth - 1
        o_f = 3 * ATTN_WIDTH
        o_u = o_f + N_HEADS
        o_g = o_u + 2 * SGU_WIDTH
        wt32 = w_in[l].T
        h, (wi2, wo2, wt, wgatet, wua, wug, wout) = _ffn(
            h, row(ffn1_norm[l]), ffn1_w_in[l], ffn1_w_out[l], fgain, final_norm=False,
            tile=TOKEN_TILE, cast=(ffn2_w_in[l], ffn2_w_out[l], (wt32, 0, o_g),
                                   (wt32, o_g, wt32.shape[0] - o_g), w_up_attn[l],
                                   w_up_sgu[l], w_out[l]))

        wft = jnp.repeat(wt[o_f:o_u], N_SPLIT, axis=0)
        wft = jnp.pad(wft, ((0, LANES - N_SPLIT * N_HEADS), (0, 0)))
        wnt = jnp.concatenate([wt[o_u:o_g], wft], axis=0)
        bf = jnp.pad(jnp.repeat(b_forget[l], N_SPLIT), (0, LANES - N_SPLIT * N_HEADS)).reshape(1, LANES)
        wsp = sgu_w_spatial[l].reshape(N_GROUPS // SGU_PACK, SGU_PACK, CHUNK, CHUNK)
        wsp = wsp.transpose(0, 2, 1, 3).reshape(N_GROUPS // SGU_PACK, CHUNK, SGU_PACK * CHUNK)
        bsp = jnp.repeat(sgu_b_spatial[l].T, GROUP_DIM, axis=1)

        q, kx, v, sgu = _proj(h, row(mix_norm[l]), wt, wnt, bf, row(sgu_ln_gain[l]),
                              row(sgu_ln_bias[l]), wsp, bsp, batch=batch, seq=seq)
        at = _attn(q, kx, v, batch=batch, seq=seq)
        h = _merge(h, at, sgu, row(mix_norm[l]), wgatet, row(b_gate[l]), wua, wug, wout,
                   seq=seq)
        h, _ = _ffn(h, row(ffn2_norm[l]), wi2, wo2, fgain, final_norm=last, tile=FFN2_TILE)
    return h.reshape(batch, seq, D_MODEL)
```

```python
import functools

import jax
import jax.numpy as jnp
from jax import lax
from jax.experimental import pallas as pl
from jax.experimental.pallas import tpu as pltpu

F32 = jnp.float32
BF16 = jnp.bfloat16

D_MODEL = 1024
D_FF = 2816
N_HEADS = 8
HEAD_DIM = 64
ATTN_WIDTH = N_HEADS * HEAD_DIM
N_GROUPS = 8
GROUP_DIM = 64
SGU_WIDTH = N_GROUPS * GROUP_DIM
CHUNK = 128
RMS_EPS = 1e-6
LN_EPS = 1e-5
LOG2E = 1.4426950408889634

LANES = 128
BF16_SUBLANES = 16
F32_SUBLANES = 8
V7X_VMEM_BYTES = 64 * 1024 * 1024
VMEM_LIMIT = V7X_VMEM_BYTES * 7 // 8

TOKEN_TILE = 512
PROJ_TILE = 1024
MERGE_TILE = 1024
FFN2_TILE = 1024
FFN_COLS = 256
ATTN_BLOCK = 256
Q_TILE = 512
LOOP_GROUPS = 2
V_ROWS = HEAD_DIM + 16
N_SPLIT = 3
QK_DEPTH = LANES
SGU_PACK = 4


def _rms_norm(x, gain):
    return (x * lax.rsqrt(jnp.mean(x * x, axis=-1, keepdims=True) + RMS_EPS)) * gain


def _gelu(x):
    return x * (lax.erf(x * (2.0 ** -0.5)) + 1.0) * 0.5


def _resident(shape):
    zeros = (0,) * len(shape)
    return pl.BlockSpec(shape, lambda *_: zeros, pipeline_mode=pl.Buffered(1))


def _params():
    return pltpu.CompilerParams(dimension_semantics=("arbitrary",),
                                vmem_limit_bytes=VMEM_LIMIT)


def _ffn_kernel(*refs, final_norm, tile, shifts):
    x_ref, gain_ref, wi_ref, wo_ref, fgain_ref = refs[:5]
    n_src = sum(2 if shift else 1 for shift in shifts)
    cast_src = list(refs[5:5 + n_src])
    o_ref = refs[5 + n_src]
    cast_dst = refs[6 + n_src:6 + n_src + len(shifts)]
    act_ref = refs[6 + n_src + len(shifts)]

    for shift, dst in zip(shifts, cast_dst):
        slab = cast_src.pop(0)[...]
        if shift:
            slab = jnp.concatenate([slab[shift:], cast_src.pop(0)[:shift, :]], axis=0)
        dst[...] = slab.astype(dst.dtype)

    wdt = wi_ref.dtype
    for s in range(tile // TOKEN_TILE):
        rows = slice(s * TOKEN_TILE, (s + 1) * TOKEN_TILE)
        x = x_ref[rows, :]
        n = _rms_norm(x, gain_ref[...]).astype(wdt)
        for j in range(D_FF // FFN_COLS):
            lo = j * FFN_COLS
            gate = jnp.dot(n, wi_ref[:, lo:lo + FFN_COLS], preferred_element_type=F32)
            up = jnp.dot(n, wi_ref[:, D_FF + lo:D_FF + lo + FFN_COLS], preferred_element_type=F32)
            act_ref[s, :, lo:lo + FFN_COLS] = (gate * jax.nn.sigmoid(gate) * up).astype(wdt)
        y = jnp.dot(act_ref[s], wo_ref[...], preferred_element_type=F32)
        h = x + 0.5 * y
        if final_norm:
            h = _rms_norm(h, fgain_ref[...])
        o_ref[rows, :] = h


def _cast_slab_rows(n_rows, n_steps):
    for rows in range(BF16_SUBLANES, n_rows + 1, BF16_SUBLANES):
        if n_rows % rows == 0 and n_rows // rows <= n_steps:
            return rows
    return pl.cdiv(pl.cdiv(n_rows, n_steps), BF16_SUBLANES) * BF16_SUBLANES


def _ffn(x, gain, wi, wo, fgain, *, final_norm, tile, cast=()):
    t = x.shape[0]
    steps = t // tile
    tok = pl.BlockSpec((tile, D_MODEL), lambda i: (i, 0))
    srcs, src_specs, dst_specs, dst_shapes, shifts = [], [], [], [], []
    for job in cast:
        arr, row0, n_rows = job if isinstance(job, tuple) else (job, 0, job.shape[0])
        rows = _cast_slab_rows(n_rows, steps)
        last = pl.cdiv(n_rows, rows) - 1
        first, shift = divmod(row0, rows)
        assert not shift or (shift % F32_SUBLANES == 0 and arr.dtype == F32), (row0, rows)
        for block in range(first, first + (2 if shift else 1)):
            srcs.append(arr)
            src_specs.append(pl.BlockSpec(
                (rows, arr.shape[1]),
                lambda i, last=last, block=block: (block + jnp.minimum(i, last), 0)))
        dst_specs.append(pl.BlockSpec((rows, arr.shape[1]),
                                      lambda i, last=last: (jnp.minimum(i, last), 0)))
        dst_shapes.append(jax.ShapeDtypeStruct((n_rows, arr.shape[1]), BF16))
        shifts.append(shift)
    step = functools.partial(_ffn_kernel, final_norm=final_norm, tile=tile, shifts=tuple(shifts))

    def streamed(x_hbm, gain_ref, wi_ref, wo_ref, fgain_ref, *rest):
        *streams, act_ref = rest
        pltpu.emit_pipeline(
            lambda x_ref, *blocks: step(x_ref, gain_ref, wi_ref, wo_ref, fgain_ref, *blocks, act_ref),
            grid=(steps,), in_specs=[tok] + src_specs, out_specs=[tok] + dst_specs,
        )(x_hbm, *streams)

    in_hbm = pl.BlockSpec(memory_space=pl.ANY)
    in_vmem = pl.BlockSpec(memory_space=pltpu.VMEM)
    outs = pl.pallas_call(
        streamed,
        in_specs=[in_hbm] + [in_vmem] * 4 + [in_hbm] * len(srcs),
        out_specs=[in_hbm] * (1 + len(dst_specs)),
        out_shape=[jax.ShapeDtypeStruct((t, D_MODEL), F32)] + dst_shapes,
        scratch_shapes=[pltpu.VMEM((tile // TOKEN_TILE, TOKEN_TILE, D_FF), wi.dtype)],
        compiler_params=pltpu.CompilerParams(vmem_limit_bytes=VMEM_LIMIT),
        name="ffn_final" if final_norm else "ffn",
    )(x, gain, wi, wo, fgain, *srcs)
    return outs[0], tuple(outs[1:])


def _split_bf16(x):
    parts = []
    for _ in range(N_SPLIT):
        p = x.astype(BF16).astype(F32)
        parts.append(p)
        x = x - p
    return parts


def _proj_kernel(h_ref, gain_ref, wqt_ref, wkt_ref, wvt_ref, wnt_ref, bf_ref, lng_ref, lnb_ref,
                 wsp_ref, bsp_ref,
                 q_ref, kx_ref, v_ref, sgu_ref, carry_ref, *, tiles_per_seq):
    i = pl.program_id(0)

    @pl.when(i % tiles_per_seq == 0)
    def _():
        carry_ref[...] = jnp.zeros_like(carry_ref)

    tdims = (((1,), (1,)), ((), ()))
    sub = ATTN_BLOCK
    sel_row = lax.broadcasted_iota(jnp.int32, (QK_DEPTH - HEAD_DIM, sub), 0)
    ones_rows = jnp.where(sel_row < N_SPLIT, 1.0, 0.0).astype(BF16)
    pad_row = lax.broadcasted_iota(jnp.int32, (V_ROWS - HEAD_DIM, sub), 0)
    ones_row = jnp.where(pad_row == 0, 1.0, 0.0).astype(BF16)
    lane = lax.broadcasted_iota(jnp.int32, (CHUNK, LANES), 1)
    term = lane % N_SPLIT
    live = lane < N_SPLIT * N_HEADS

    def by_term(parts):
        out = parts[-1]
        for j in range(N_SPLIT - 2, -1, -1):
            out = jnp.where(term == j, parts[j], out)
        return out

    row = lax.broadcasted_iota(jnp.int32, (CHUNK, CHUNK), 0)
    col = lax.broadcasted_iota(jnp.int32, (CHUNK, CHUNK), 1)
    tri = jnp.where(row >= col, 1.0, 0.0).astype(BF16)
    pack_w = SGU_PACK * GROUP_DIM
    wrow = lax.broadcasted_iota(jnp.int32, (CHUNK, SGU_PACK * CHUNK), 0)
    wcol = lax.broadcasted_iota(jnp.int32, (CHUNK, SGU_PACK * CHUNK), 1) % CHUNK
    wsp = [jnp.where(wcol <= wrow, wsp_ref[p], 0.0).astype(BF16)
           for p in range(N_GROUPS // SGU_PACK)]
    lane_group = lax.broadcasted_iota(jnp.int32, (CHUNK, pack_w), 1) // GROUP_DIM

    def project(s):
        r0 = s * sub
        nb = _rms_norm(h_ref[r0:r0 + sub, :], gain_ref[...]).astype(BF16)
        qt = lax.dot_general(wqt_ref[...], nb, tdims, preferred_element_type=F32)
        qt = (qt * (LOG2E * HEAD_DIM ** -0.5)).astype(BF16)
        vt = lax.dot_general(wvt_ref[...], nb, tdims, preferred_element_type=F32).astype(BF16)
        for hd in range(N_HEADS):
            q_ref[0, hd, :HEAD_DIM, r0:r0 + sub] = qt[hd * HEAD_DIM:(hd + 1) * HEAD_DIM, :]
            q_ref[0, hd, HEAD_DIM:, r0:r0 + sub] = ones_rows
            v_ref[0, hd, s, :HEAD_DIM, :] = vt[hd * HEAD_DIM:(hd + 1) * HEAD_DIM, :]
            v_ref[0, hd, s, HEAD_DIM:, :] = ones_row
        return (lax.dot_general(nb, wkt_ref[...], tdims, preferred_element_type=F32),
                lax.dot_general(nb, wnt_ref[...], tdims, preferred_element_type=F32))

    def tail(s, zs, carry):
        r0 = s * sub
        zk, z = zs
        u = z[:, :SGU_WIDTH]
        vg = z[:, SGU_WIDTH:2 * SGU_WIDTH]
        f = z[:, 2 * SGU_WIDTH:] + bf_ref[...]
        logf = jnp.minimum(f, 0.0) - jnp.log1p(jnp.exp(-jnp.abs(f)))

        gu = _gelu(u)
        gv = _gelu(vg)
        mu = jnp.mean(gv, axis=-1, keepdims=True)
        dv = gv - mu
        var = jnp.mean(dv * dv, axis=-1, keepdims=True)
        vn = ((dv * lax.rsqrt(var + LN_EPS)) * lng_ref[...] + lnb_ref[...]).astype(BF16)

        for c in range(sub // CHUNK):
            rows = slice(c * CHUNK, (c + 1) * CHUNK)
            out_rows = slice(r0 + c * CHUNK, r0 + (c + 1) * CHUNK)
            local = jnp.dot(tri, by_term(_split_bf16(logf[rows])).astype(BF16),
                            preferred_element_type=F32)
            shifted = {0: local}
            for d in range(1, N_SPLIT):
                shifted[d] = pltpu.roll(local, d, 1)
                shifted[-d] = pltpu.roll(local, LANES - d, 1)
            csum = carry
            for j in range(N_SPLIT):
                csum = csum + by_term([shifted[t - j] for t in range(N_SPLIT)])
            carry = csum[CHUNK - 1:, :]
            bias = jnp.where(live, by_term(_split_bf16(csum * -LOG2E)), 0.0)
            for hd in range(N_HEADS):
                pair = hd // 2
                kslab = zk[rows, pair * LANES:(pair + 1) * LANES]
                if hd % 2:
                    kslab = pltpu.roll(kslab, HEAD_DIM, 1)
                hbias = pltpu.roll(bias, HEAD_DIM - N_SPLIT * hd, 1)
                kx = jnp.where(lane < HEAD_DIM, kslab,
                               jnp.where(lane < HEAD_DIM + N_SPLIT, hbias, 0.0))
                kx_ref[0, hd, out_rows, :] = kx.astype(BF16)

            mixed = []
            for p in range(N_GROUPS // SGU_PACK):
                slab = vn[rows, p * pack_w:(p + 1) * pack_w]
                blocks = jnp.concatenate(
                    [jnp.where(lane_group == r, slab, jnp.zeros_like(slab))
                     for r in range(SGU_PACK)], axis=0)
                mixed.append(jnp.dot(wsp[p], blocks, preferred_element_type=F32))
            mixed = jnp.concatenate(mixed, axis=1) + bsp_ref[...]
            sgu_ref[out_rows, :] = (gu[rows] * mixed).astype(BF16)
        return carry

    n_sub = PROJ_TILE // sub
    carry = carry_ref[...]
    z_prev = project(0)
    for s in range(1, n_sub):
        z = project(s)
        carry = tail(s - 1, z_prev, carry)
        z_prev = z
    carry_ref[...] = tail(n_sub - 1, z_prev, carry)


def _proj(h, gain, wt, wnt, bf, lng, lnb, wsp, bsp, *, batch, seq):
    t = batch * seq
    tiles_per_seq = seq // PROJ_TILE
    sub = PROJ_TILE // ATTN_BLOCK
    tok = lambda i: (i, 0)
    wblock = lambda n: pl.BlockSpec((ATTN_WIDTH, D_MODEL), lambda i: (n, 0),
                                    pipeline_mode=pl.Buffered(1))
    return pl.pallas_call(
        functools.partial(_proj_kernel, tiles_per_seq=tiles_per_seq),
        grid=(t // PROJ_TILE,),
        in_specs=[pl.BlockSpec((PROJ_TILE, D_MODEL), tok), _resident((1, D_MODEL)),
                  wblock(0), wblock(1), wblock(2), _resident(wnt.shape),
                  _resident((1, LANES)), _resident((1, SGU_WIDTH)), _resident((1, SGU_WIDTH)),
                  _resident(wsp.shape), _resident(bsp.shape)],
        out_specs=[pl.BlockSpec((1, N_HEADS, QK_DEPTH, PROJ_TILE),
                                lambda i: (i // tiles_per_seq, 0, 0, i % tiles_per_seq)),
                   pl.BlockSpec((1, N_HEADS, PROJ_TILE, LANES),
                                lambda i: (i // tiles_per_seq, 0, i % tiles_per_seq, 0)),
                   pl.BlockSpec((1, N_HEADS, sub, V_ROWS, ATTN_BLOCK),
                                lambda i: (i // tiles_per_seq, 0, i % tiles_per_seq, 0, 0)),
                   pl.BlockSpec((PROJ_TILE, SGU_WIDTH), tok)],
        out_shape=[jax.ShapeDtypeStruct((batch, N_HEADS, QK_DEPTH, seq), BF16),
                   jax.ShapeDtypeStruct((batch, N_HEADS, seq, LANES), BF16),
                   jax.ShapeDtypeStruct((batch, N_HEADS, seq // ATTN_BLOCK, V_ROWS, ATTN_BLOCK), BF16),
                   jax.ShapeDtypeStruct((t, SGU_WIDTH), BF16)],
        scratch_shapes=[pltpu.VMEM((1, LANES), F32)],
        compiler_params=_params(),
        name="proj",
    )(h, gain, wt, wt, wt, wnt, bf, lng, lnb, wsp, bsp)


def _attn_kernel(q_ref, qnext_ref, kx_ref, v_ref, o_ref, s_ref, smax_ref, m_ref, acc_ref, *, nblk):
    blk = ATTN_BLOCK
    n_diag = Q_TILE // blk
    qi = pl.program_id(1)
    n_off = qi * n_diag

    m_ref[...] = jnp.full(m_ref.shape, -jnp.inf, F32)
    acc_ref[...] = jnp.zeros(acc_ref.shape, F32)

    key_pos = lax.broadcasted_iota(jnp.int32, (blk, Q_TILE), 0)
    qry_pos = lax.broadcasted_iota(jnp.int32, (blk, Q_TILE), 1)
    causal = key_pos <= qry_pos

    def scores(j, hd, diag, qt_ref=q_ref):
        rows = pl.ds(j * blk if isinstance(j, int) else pl.multiple_of(j * blk, blk), blk)
        lo = 0 if diag is None else diag * blk
        st = jnp.dot(kx_ref[0, hd, rows, :], qt_ref[0, hd, :, lo:], preferred_element_type=F32)
        if diag is not None:
            st = jnp.where(causal[:, :Q_TILE - lo], st, -jnp.inf)
        s_ref[hd, :, lo:] = st
        smax_ref[hd, :, lo:] = jnp.max(st, axis=0, keepdims=True)

    def step(j, j_next, next_diag=None, next_q_ref=q_ref, lo=0):
        for hd in range(N_HEADS):
            st = s_ref[hd, :, lo:]
            m_old = m_ref[hd, :, lo:]
            m_new = jnp.maximum(m_old, smax_ref[hd, :, lo:])
            scores(j_next, hd, next_diag, next_q_ref)
            alpha = jnp.exp2(m_old - m_new)
            pt = jnp.exp2(st - m_new).astype(BF16)
            pv = jnp.dot(v_ref[0, hd, j], pt, preferred_element_type=F32)
            acc_ref[hd, :, lo:] = acc_ref[hd, :, lo:] * alpha + pv
            m_ref[hd, :, lo:] = m_new

    @pl.when(qi == 0)
    def _():
        for hd in range(N_HEADS):
            scores(n_off, hd, 0)

    for d in range(1, n_diag):
        step(n_off + d - 1, n_off + d, d, lo=(d - 1) * blk)
    last_diag = n_off + n_diag - 1
    last_lo = (n_diag - 1) * blk

    @pl.when(qi == 0)
    def _():
        step(last_diag, min(n_diag, nblk - n_diag), 0, qnext_ref, lo=last_lo)

    def group(i):
        for d in range(n_diag):
            step(i * n_diag + d - 1, i * n_diag + d)

    def body(t, carry):
        for g in range(LOOP_GROUPS):
            group(1 + t * LOOP_GROUPS + g)
        return carry

    @pl.when(qi > 0)
    def _():
        step(last_diag, 0, lo=last_lo)
        for d in range(1, n_diag):
            step(d - 1, d)
        n_body = (qi - 1) // LOOP_GROUPS
        lax.fori_loop(0, n_body, body, 0)
        for g in range(LOOP_GROUPS - 1):
            @pl.when(n_body * LOOP_GROUPS + g < qi - 1)
            def _():
                group(1 + n_body * LOOP_GROUPS + g)
        step(n_off - 1, jnp.minimum(n_off + n_diag, nblk - n_diag), 0, qnext_ref)

    for hd in range(N_HEADS):
        acc = acc_ref[hd]
        o_ref[0, hd * HEAD_DIM:(hd + 1) * HEAD_DIM, :] = (
            acc[:HEAD_DIM] / acc[HEAD_DIM:HEAD_DIM + 1]).astype(BF16)


def _attn(q, kx, v, *, batch, seq):
    nblk = seq // ATTN_BLOCK
    n_tiles = seq // Q_TILE
    q_spec = lambda tile: pl.BlockSpec((1, N_HEADS, QK_DEPTH, Q_TILE),
                                       lambda b, i: (b, 0, 0, tile(i)))
    return pl.pallas_call(
        functools.partial(_attn_kernel, nblk=nblk),
        grid=(batch, n_tiles),
        in_specs=[q_spec(lambda i: i), q_spec(lambda i: jnp.minimum(i + 1, n_tiles - 1)),
                  pl.BlockSpec((1, N_HEADS, seq, LANES), lambda b, i: (b, 0, 0, 0)),
                  pl.BlockSpec((1, N_HEADS, nblk, V_ROWS, ATTN_BLOCK), lambda b, i: (b, 0, 0, 0, 0))],
        out_specs=pl.BlockSpec((1, ATTN_WIDTH, Q_TILE), lambda b, i: (b, 0, i)),
        out_shape=jax.ShapeDtypeStruct((batch, ATTN_WIDTH, seq), BF16),
        scratch_shapes=[pltpu.VMEM((N_HEADS, ATTN_BLOCK, Q_TILE), F32),
                        pltpu.VMEM((N_HEADS, 1, Q_TILE), F32),
                        pltpu.VMEM((N_HEADS, 1, Q_TILE), F32),
                        pltpu.VMEM((N_HEADS, V_ROWS, Q_TILE), F32)],
        compiler_params=pltpu.CompilerParams(dimension_semantics=("arbitrary", "arbitrary"),
                                             vmem_limit_bytes=VMEM_LIMIT),
        name="attn",
    )(q, q, kx, v)


def _merge_kernel(h_ref, at_ref, sgu_ref, gain_ref, wgatet_ref, bgate_ref, wua_ref, wug_ref,
                  wout_ref, o_ref):
    for s in range(MERGE_TILE // TOKEN_TILE):
        rows = slice(s * TOKEN_TILE, (s + 1) * TOKEN_TILE)
        h = h_ref[rows, :]
        nb = _rms_norm(h, gain_ref[...]).astype(BF16)
        logits = lax.dot_general(nb, wgatet_ref[...], (((1,), (1,)), ((), ())),
                                 preferred_element_type=F32)
        gates = jax.nn.sigmoid(logits + bgate_ref[...])
        up_a = lax.dot_general(at_ref[0, :, rows], wua_ref[...], (((0,), (0,)), ((), ())),
                               preferred_element_type=F32)
        up_g = jnp.dot(sgu_ref[rows, :], wug_ref[...], preferred_element_type=F32)
        merged = gates[:, :D_MODEL] * up_a + gates[:, D_MODEL:] * up_g
        o_ref[rows, :] = h + jnp.dot(merged.astype(BF16), wout_ref[...],
                                     preferred_element_type=F32)


def _merge(h, at, sgu, gain, wgate, bgate, wua, wug, wout, *, seq):
    t = h.shape[0]
    tiles_per_seq = seq // MERGE_TILE
    tok = lambda i: (i, 0)
    return pl.pallas_call(
        _merge_kernel,
        grid=(t // MERGE_TILE,),
        in_specs=[pl.BlockSpec((MERGE_TILE, D_MODEL), tok),
                  pl.BlockSpec((1, ATTN_WIDTH, MERGE_TILE),
                               lambda i: (i // tiles_per_seq, 0, i % tiles_per_seq)),
                  pl.BlockSpec((MERGE_TILE, SGU_WIDTH), tok),
                  _resident((1, D_MODEL)), _resident(wgate.shape), _resident(bgate.shape),
                  _resident(wua.shape), _resident(wug.shape), _resident(wout.shape)],
        out_specs=pl.BlockSpec((MERGE_TILE, D_MODEL), tok),
        out_shape=jax.ShapeDtypeStruct((t, D_MODEL), F32),
        compiler_params=_params(),
        name="merge",
    )(h, at, sgu, gain, wgate, bgate, wua, wug, wout)


def kernel(x, ffn1_norm, ffn1_w_in, ffn1_w_out, mix_norm, w_in, b_forget, b_gate, sgu_ln_gain,
           sgu_ln_bias, sgu_w_spatial, sgu_b_spatial, w_up_attn, w_up_sgu, w_out, ffn2_norm,
           ffn2_w_in, ffn2_w_out, final_norm):
    batch, seq, _ = x.shape
    depth = ffn1_norm.shape[0]
    row = lambda v: v.reshape(1, -1)
    fgain = row(final_norm)
    h = x.reshape(batch * seq, D_MODEL)
    for l in range(depth):
        last = l == depth - 1
        o_f = 3 * ATTN_WIDTH
        o_u = o_f + N_HEADS
        o_g = o_u + 2 * SGU_WIDTH
        wt32 = w_in[l].T
        h, (wi2, wo2, wt, wgatet, wua, wug, wout) = _ffn(
            h, row(ffn1_norm[l]), ffn1_w_in[l], ffn1_w_out[l], fgain, final_norm=False,
            tile=TOKEN_TILE, cast=(ffn2_w_in[l], ffn2_w_out[l], (wt32, 0, o_g),
                                   (wt32, o_g, wt32.shape[0] - o_g), w_up_attn[l],
                                   w_up_sgu[l], w_out[l]))

        wft = jnp.repeat(wt[o_f:o_u], N_SPLIT, axis=0)
        wft = jnp.pad(wft, ((0, LANES - N_SPLIT * N_HEADS), (0, 0)))
        wnt = jnp.concatenate([wt[o_u:o_g], wft], axis=0)
        bf = jnp.pad(jnp.repeat(b_forget[l], N_SPLIT), (0, LANES - N_SPLIT * N_HEADS)).reshape(1, LANES)
        wsp = sgu_w_spatial[l].reshape(N_GROUPS // SGU_PACK, SGU_PACK, CHUNK, CHUNK)
        wsp = wsp.transpose(0, 2, 1, 3).reshape(N_GROUPS // SGU_PACK, CHUNK, SGU_PACK * CHUNK)
        bsp = jnp.repeat(sgu_b_spatial[l].T, GROUP_DIM, axis=1)

        q, kx, v, sgu = _proj(h, row(mix_norm[l]), wt, wnt, bf, row(sgu_ln_gain[l]),
                              row(sgu_ln_bias[l]), wsp, bsp, batch=batch, seq=seq)
        at = _attn(q, kx, v, batch=batch, seq=seq)
        h = _merge(h, at, sgu, row(mix_norm[l]), wgatet, row(b_gate[l]), wua, wug, wout,
                   seq=seq)
        h, _ = _ffn(h, row(ffn2_norm[l]), wi2, wo2, fgain, final_norm=last, tile=FFN2_TILE)
    return h.reshape(batch, seq, D_MODEL)
```
